```python
import math
import jax, jax.numpy as jnp
from jax import lax
import numpy as np

D_MODEL = 4096
BATCH = 2
SEQ = 8192
DEPTH = 2

CHUNK = 64
LEFT_CHUNKS = 8
BAND = (LEFT_CHUNKS + 1) * CHUNK
D_ATTN = D_MODEL // 2
DH_ATTN = 128
H_ATTN = D_ATTN // DH_ATTN
MAX_REL = 256
N_REL = 2 * MAX_REL + 1
D_SSM = D_MODEL - D_ATTN
SSM_GROUP = 16
N_GROUPS = D_SSM // SSM_GROUP
SSM_STATE = 64
DK_DELTA = 128
DV_DELTA = 128
H_DELTA = D_MODEL // DV_DELTA
D_DELTA = H_DELTA * DV_DELTA
CONV_K = 4
D_FF = -(-8 * D_MODEL // (3 * 256)) * 256
N_EVEN = (DEPTH + 1) // 2
N_ODD = DEPTH // 2
ALPHA = (2.0 * DEPTH) ** 0.25
BETA = (8.0 * DEPTH) ** -0.25
LN_EPS = 1e-5
NORM_EPS = 1e-6
NEG_BIG = -1e30

kernel_name = 'hybrid_streaming_relattn_s5_gdn'


def layer_norm(x, gain, bias):
    xf = x.astype(jnp.float32)
    mu = jnp.mean(xf, axis=-1, keepdims=True)
    var = jnp.mean(jnp.square(xf - mu), axis=-1, keepdims=True)
    return ((xf - mu) * lax.rsqrt(var + LN_EPS) * gain + bias).astype(x.dtype)


def swiglu(x, w_gate, w_up, w_down):
    return (jax.nn.silu(x @ w_gate) * (x @ w_up)) @ w_down


def chunked_relpos_attention(q, k, v, rel_bias):
    b, l, h, dh = q.shape
    n = l // CHUNK
    left = LEFT_CHUNKS * CHUNK
    k_pad = jnp.pad(k, ((0, 0), (left, 0), (0, 0), (0, 0)))
    v_pad = jnp.pad(v, ((0, 0), (left, 0), (0, 0), (0, 0)))
    q_chunks = jnp.moveaxis(q.reshape(b, n, CHUNK, h, dh), 1, 0)
    band = jnp.arange(BAND)
    rel = jnp.arange(CHUNK)[:, None] + left - band[None, :]
    bias = rel_bias[:, jnp.clip(rel, -MAX_REL, MAX_REL) + MAX_REL].astype(jnp.float32)
    scale = dh ** -0.5

    def one_chunk(args):
        c, q_c = args
        start = c * CHUNK
        k_b = lax.dynamic_slice_in_dim(k_pad, start, BAND, axis=1)
        v_b = lax.dynamic_slice_in_dim(v_pad, start, BAND, axis=1)
        s = jnp.einsum('bqhd,bkhd->bhqk', q_c, k_b).astype(jnp.float32) * scale + bias
        valid = (start + band) >= left
        p = jax.nn.softmax(jnp.where(valid, s, NEG_BIG), axis=-1)
        return jnp.einsum('bhqk,bkhd->bqhd', p.astype(v.dtype), v_b)

    out = lax.map(one_chunk, (jnp.arange(n), q_chunks))
    return jnp.moveaxis(out, 0, 1).reshape(b, l, h * dh)


def s5_ssm(u, lam_re, lam_im, log_step, b_re, b_im, c_re, c_im, d_skip):
    f32 = jnp.float32
    lam = lax.complex(jnp.minimum(lam_re.astype(f32), -1e-4), lam_im.astype(f32))
    step = jnp.exp(log_step.astype(f32))[:, None]
    lam_bar = jnp.exp(lam * step)
    b_bar = ((lam_bar - 1.0) / lam)[..., None] * lax.complex(b_re.astype(f32), b_im.astype(f32))
    c_mat = lax.complex(c_re.astype(f32), c_im.astype(f32))
    uf = u.astype(f32)
    bu = jnp.einsum('blgh,gph->lbgp', uf.astype(jnp.complex64), b_bar)
    decay = jnp.broadcast_to(lam_bar[None, None], (uf.shape[1], 1) + lam_bar.shape)

    def combine(e1, e2):
        a1, x1 = e1
        a2, x2 = e2
        return a1 * a2, a2 * x1 + x2

    _, states = lax.associative_scan(combine, (decay, bu), axis=0)
    y = jnp.real(jnp.einsum('lbgp,ghp->blgh', states, c_mat)) + d_skip.astype(f32) * uf
    return y.astype(u.dtype)


def causal_depthwise_conv(x, w):
    kw, ch = w.shape
    return lax.conv_general_dilated(
        x, w[:, None, :].astype(x.dtype), window_strides=(1,), padding=[(kw - 1, 0)],
        dimension_numbers=('NWC', 'WIO', 'NWC'), feature_group_count=ch)


def l2_normalize(t):
    return t * lax.rsqrt(jnp.sum(jnp.square(t), axis=-1, keepdims=True) + NORM_EPS)


def chunk_gated_delta_rule(q, k, v, g, beta):
    b, l, h, dk = q.shape
    dv = v.shape[-1]
    n = l // CHUNK

    def chunks(t):
        t = t.reshape((b, n, CHUNK, h) + t.shape[3:])
        return jnp.moveaxis(jnp.swapaxes(t, 2, 3), 1, 0)

    q = chunks(q) * dk ** -0.5
    k = chunks(k)
    v = chunks(v)
    beta = chunks(beta)
    g = jnp.cumsum(chunks(g), axis=-1)
    causal = jnp.tril(jnp.ones((CHUNK, CHUNK), bool))
    strict = jnp.tril(jnp.ones((CHUNK, CHUNK), bool), -1)
    decay = jnp.exp(jnp.where(causal, g[..., :, None] - g[..., None, :], -jnp.inf))
    k_beta = k * beta[..., None]
    lower = jnp.where(strict, jnp.einsum('nbhcd,nbhsd->nbhcs', k_beta, k) * decay, 0.0) + jnp.eye(CHUNK, dtype=q.dtype)
    rhs = jnp.concatenate([v * beta[..., None], k_beta * jnp.exp(g)[..., None]], axis=-1)
    sol = lax.linalg.triangular_solve(lower, rhs, left_side=True, lower=True, unit_diagonal=True)
    u_c, w_c = sol[..., :dv], sol[..., dv:]
    attn = jnp.where(causal, jnp.einsum('nbhcd,nbhsd->nbhcs', q, k) * decay, 0.0)

    def step(state, inp):
        q_i, k_i, u_i, w_i, attn_i, g_i = inp
        v_new = u_i - jnp.einsum('bhcd,bhde->bhce', w_i, state)
        out = (jnp.einsum('bhcd,bhde->bhce', q_i * jnp.exp(g_i)[..., None], state)
               + jnp.einsum('bhcs,bhse->bhce', attn_i, v_new))
        g_last = g_i[..., -1:]
        state = (state * jnp.exp(g_last)[..., None]
                 + jnp.einsum('bhcd,bhce->bhde', k_i * jnp.exp(g_last - g_i)[..., None], v_new))
        return state, out

    state0 = jnp.zeros((b, h, dk, dv), q.dtype)
    _, out = lax.scan(step, state0, (q, k, u_c, w_c, attn, g))
    return jnp.swapaxes(jnp.moveaxis(out, 0, 1), 2, 3).reshape(b, l, h, dv)


def mixer_attn_ssm(x, w_in, rel_bias, lam_re, lam_im, log_step, b_re, b_im, c_re, c_im,
                   d_skip, w_glu, b_glu, w_out):
    b, l, _ = x.shape
    proj = x @ w_in
    q, k, v, u = jnp.split(proj, [D_ATTN, 2 * D_ATTN, 3 * D_ATTN], axis=-1)
    heads = lambda t: t.reshape(b, l, H_ATTN, DH_ATTN)
    y_a = chunked_relpos_attention(heads(q), heads(k), heads(v), rel_bias)
    y_b = s5_ssm(u.reshape(b, l, N_GROUPS, SSM_GROUP), lam_re, lam_im, log_step,
                 b_re, b_im, c_re, c_im, d_skip).reshape(b, l, D_SSM)
    y_b = jax.nn.gelu(y_b)
    y_b = y_b * jax.nn.sigmoid(y_b @ w_glu + b_glu)
    return jnp.concatenate([y_a, y_b], axis=-1) @ w_out


def mixer_gated_deltanet(x, w_in, conv_w, a_log, dt_bias, norm_w, w_out):
    b, l, _ = x.shape
    f32 = jnp.float32
    proj = x @ w_in
    qkv, gate, beta_raw, a_raw = jnp.split(
        proj, [3 * D_DELTA, 4 * D_DELTA, 4 * D_DELTA + H_DELTA], axis=-1)
    qkv = jax.nn.silu(causal_depthwise_conv(qkv, conv_w))
    q, k, v = jnp.split(qkv.astype(f32), 3, axis=-1)
    q = l2_normalize(q.reshape(b, l, H_DELTA, DK_DELTA))
    k = l2_normalize(k.reshape(b, l, H_DELTA, DK_DELTA))
    v = v.reshape(b, l, H_DELTA, DV_DELTA)
    beta = jax.nn.sigmoid(beta_raw.astype(f32))
    g = -jnp.exp(a_log.astype(f32)) * jax.nn.softplus(a_raw.astype(f32) + dt_bias.astype(f32))
    o = chunk_gated_delta_rule(q, k, v, g, beta)
    o = o * lax.rsqrt(jnp.mean(jnp.square(o), axis=-1, keepdims=True) + NORM_EPS) * norm_w.astype(f32)
    o = o.astype(x.dtype) * jax.nn.silu(gate.reshape(b, l, H_DELTA, DV_DELTA))
    return o.reshape(b, l, D_DELTA) @ w_out


def setup_inputs(seed: int = 0) -> dict:
    key = jax.random.key(seed)
    ks = iter(jax.random.split(key, 32))
    f32 = jnp.float32
    nrm = lambda shape, scale: jax.random.normal(next(ks), shape, f32) * scale
    x = nrm((BATCH, SEQ, D_MODEL), 1.0)
    ab_w_in = nrm((N_EVEN, D_MODEL, 3 * D_ATTN + D_SSM), D_MODEL ** -0.5)
    ab_rel_bias = nrm((N_EVEN, H_ATTN, N_REL), 0.1)
    ab_lam_re = -0.5 + nrm((N_EVEN, N_GROUPS, SSM_STATE), 0.01)
    ab_lam_im = (jnp.broadcast_to(jnp.pi * jnp.arange(SSM_STATE, dtype=f32), (N_EVEN, N_GROUPS, SSM_STATE))
                 + nrm((N_EVEN, N_GROUPS, SSM_STATE), 0.01))
    ab_log_step = jax.random.uniform(next(ks), (N_EVEN, N_GROUPS), f32, math.log(1e-3), math.log(1e-1))
    ab_b_re = nrm((N_EVEN, N_GROUPS, SSM_STATE, SSM_GROUP), (2 * SSM_GROUP) ** -0.5)
    ab_b_im = nrm((N_EVEN, N_GROUPS, SSM_STATE, SSM_GROUP), (2 * SSM_GROUP) ** -0.5)
    ab_c_re = nrm((N_EVEN, N_GROUPS, SSM_GROUP, SSM_STATE), SSM_STATE ** -0.5)
    ab_c_im = nrm((N_EVEN, N_GROUPS, SSM_GROUP, SSM_STATE), SSM_STATE ** -0.5)
    ab_d_skip = nrm((N_EVEN, N_GROUPS, SSM_GROUP), 1.0)
    ab_w_glu = nrm((N_EVEN, D_SSM, D_SSM), D_SSM ** -0.5)
    ab_b_glu = nrm((N_EVEN, D_SSM), 0.01)
    ab_w_out = nrm((N_EVEN, D_ATTN + D_SSM, D_MODEL), BETA * (D_ATTN + D_SSM) ** -0.5)
    c_w_in = nrm((N_ODD, D_MODEL, 4 * D_DELTA + 2 * H_DELTA), D_MODEL ** -0.5)
    c_conv = nrm((N_ODD, CONV_K, 3 * D_DELTA), CONV_K ** -0.5)
    c_a_log = jnp.log(jax.random.uniform(next(ks), (N_ODD, H_DELTA), f32, 1.0, 16.0))
    dt = jnp.exp(jax.random.uniform(next(ks), (N_ODD, H_DELTA), f32, math.log(1e-3), math.log(1e-1)))
    c_dt_bias = dt + jnp.log(-jnp.expm1(-dt))
    c_norm_w = 1.0 + nrm((N_ODD, DV_DELTA), 0.01)
    c_w_out = nrm((N_ODD, D_DELTA, D_MODEL), BETA * D_DELTA ** -0.5)
    ffn_w_gate = nrm((DEPTH, D_MODEL, D_FF), D_MODEL ** -0.5)
    ffn_w_up = nrm((DEPTH, D_MODEL, D_FF), D_MODEL ** -0.5)
    ffn_w_down = nrm((DEPTH, D_FF, D_MODEL), BETA * D_FF ** -0.5)
    ln_gain = 1.0 + nrm((DEPTH, 2, D_MODEL), 0.01)
    ln_bias = nrm((DEPTH, 2, D_MODEL), 0.01)
    return {'x': x, 'ab_w_in': ab_w_in, 'ab_rel_bias': ab_rel_bias, 'ab_lam_re': ab_lam_re,
            'ab_lam_im': ab_lam_im, 'ab_log_step': ab_log_step, 'ab_b_re': ab_b_re, 'ab_b_im': ab_b_im,
            'ab_c_re': ab_c_re, 'ab_c_im': ab_c_im, 'ab_d_skip': ab_d_skip, 'ab_w_glu': ab_w_glu,
            'ab_b_glu': ab_b_glu, 'ab_w_out': ab_w_out, 'c_w_in': c_w_in, 'c_conv': c_conv,
            'c_a_log': c_a_log, 'c_dt_bias': c_dt_bias, 'c_norm_w': c_norm_w, 'c_w_out': c_w_out,
            'ffn_w_gate': ffn_w_gate, 'ffn_w_up': ffn_w_up, 'ffn_w_down': ffn_w_down,
            'ln_gain': ln_gain, 'ln_bias': ln_bias}


def reference(x, ab_w_in, ab_rel_bias, ab_lam_re, ab_lam_im, ab_log_step, ab_b_re, ab_b_im,
              ab_c_re, ab_c_im, ab_d_skip, ab_w_glu, ab_b_glu, ab_w_out, c_w_in, c_conv,
              c_a_log, c_dt_bias, c_norm_w, c_w_out, ffn_w_gate, ffn_w_up, ffn_w_down,
              ln_gain, ln_bias):
    for layer in range(DEPTH):
        i = layer // 2
        if layer % 2 == 0:
            mix = mixer_attn_ssm(x, ab_w_in[i], ab_rel_bias[i], ab_lam_re[i], ab_lam_im[i],
                                 ab_log_step[i], ab_b_re[i], ab_b_im[i], ab_c_re[i], ab_c_im[i],
                                 ab_d_skip[i], ab_w_glu[i], ab_b_glu[i], ab_w_out[i])
        else:
            mix = mixer_gated_deltanet(x, c_w_in[i], c_conv[i], c_a_log[i], c_dt_bias[i],
                                       c_norm_w[i], c_w_out[i])
        x = layer_norm(ALPHA * x + mix, ln_gain[layer, 0], ln_bias[layer, 0])
        x = layer_norm(ALPHA * x + swiglu(x, ffn_w_gate[layer], ffn_w_up[layer], ffn_w_down[layer]),
                       ln_gain[layer, 1], ln_bias[layer, 1])
    return x
```

```python
import functools
import math

import jax
import jax.numpy as jnp
from jax import lax
from jax.experimental import pallas as pl
from jax.experimental.pallas import tpu as pltpu

F32 = jnp.float32
BF16 = jnp.bfloat16

CHUNK = 64
LEFT_CHUNKS = 8
DH_ATTN = 128
MAX_REL = 256
SSM_GROUP = 16
SSM_STATE = 64
DK_DELTA = 128
CONV_K = 4
DEPTH = 2
ALPHA = (2.0 * DEPTH) ** 0.25
LN_EPS = 1e-5
NORM_EPS = 1e-6
NEG_BIG = -1e30

V7X_VMEM_LIMIT_BYTES = 56 * 1024 * 1024
LANES = 128
SSM_BLOCK = 16
HI = lax.Precision.HIGHEST


def _cparams(n_axes):
    return pltpu.CompilerParams(
        dimension_semantics=("arbitrary",) * n_axes,
        vmem_limit_bytes=V7X_VMEM_LIMIT_BYTES,
    )


def _sigmoid(x):
    return 1.0 / (1.0 + jnp.exp(-x))


def _mm_kernel(x_ref, w_ref, o_ref):
    o_ref[...] = jnp.dot(x_ref[...], w_ref[...], preferred_element_type=F32).astype(o_ref.dtype)


def matmul(x, w, out_dtype, *, tm, tn, name):
    m, k = x.shape
    n = w.shape[1]
    assert m % tm == 0 and n % tn == 0
    return pl.pallas_call(
        _mm_kernel,
        grid=(m // tm, n // tn),
        in_specs=[
            pl.BlockSpec((tm, k), lambda i, j: (i, 0)),
            pl.BlockSpec((k, tn), lambda i, j: (0, j)),
        ],
        out_specs=pl.BlockSpec((tm, tn), lambda i, j: (i, j)),
        out_shape=jax.ShapeDtypeStruct((m, n), out_dtype),
        compiler_params=_cparams(2),
        name=name,
    )(x, w)


def _mm_resid_kernel(x_ref, w_ref, r_ref, o_ref, acc_ref, *, alpha, nk):
    kk = pl.program_id(2)

    @pl.when(kk == 0)
    def _():
        acc_ref[...] = alpha * r_ref[...]

    acc_ref[...] += jnp.dot(x_ref[...], w_ref[...], preferred_element_type=F32)

    @pl.when(kk == nk - 1)
    def _():
        o_ref[...] = acc_ref[...]


def matmul_resid(x, w, resid, alpha, *, tm, tn, tk, name):
    m, k = x.shape
    n = w.shape[1]
    assert m % tm == 0 and n % tn == 0 and k % tk == 0
    nk = k // tk
    return pl.pallas_call(
        functools.partial(_mm_resid_kernel, alpha=alpha, nk=nk),
        grid=(m // tm, n // tn, nk),
        in_specs=[
            pl.BlockSpec((tm, tk), lambda i, j, kk: (i, kk)),
            pl.BlockSpec((tk, tn), lambda i, j, kk: (kk, j)),
            pl.BlockSpec((tm, tn), lambda i, j, kk: (i, j)),
        ],
        out_specs=pl.BlockSpec((tm, tn), lambda i, j, kk: (i, j)),
        out_shape=jax.ShapeDtypeStruct((m, n), F32),
        scratch_shapes=[pltpu.VMEM((tm, tn), F32)],
        compiler_params=_cparams(3),
        name=name,
    )(x, w, resid)


def _swiglu_kernel(x_ref, wg_ref, wu_ref, o_ref):
    x = x_ref[...]
    g = jnp.dot(x, wg_ref[...], preferred_element_type=F32)
    u = jnp.dot(x, wu_ref[...], preferred_element_type=F32)
    o_ref[...] = (g * _sigmoid(g) * u).astype(o_ref.dtype)


def swiglu_up(x, wg, wu, *, tm, tn, name):
    m, k = x.shape
    n = wg.shape[1]
    assert m % tm == 0 and n % tn == 0
    return pl.pallas_call(
        _swiglu_kernel,
        grid=(m // tm, n // tn),
        in_specs=[
            pl.BlockSpec((tm, k), lambda i, j: (i, 0)),
            pl.BlockSpec((k, tn), lambda i, j: (0, j)),
            pl.BlockSpec((k, tn), lambda i, j: (0, j)),
        ],
        out_specs=pl.BlockSpec((tm, tn), lambda i, j: (i, j)),
        out_shape=jax.ShapeDtypeStruct((m, n), BF16),
        compiler_params=_cparams(2),
        name=name,
    )(x, wg, wu)


def _ln_kernel(y_ref, g_ref, b_ref, o32_ref, o16_ref):
    y = y_ref[...]
    mu = jnp.mean(y, axis=-1, keepdims=True)
    yc = y - mu
    var = jnp.mean(yc * yc, axis=-1, keepdims=True)
    x = yc * lax.rsqrt(var + LN_EPS) * g_ref[...] + b_ref[...]
    o32_ref[...] = x
    o16_ref[...] = x.astype(BF16)


def layer_norm(y, gain, bias, *, tm, name):
    m, d = y.shape
    assert m % tm == 0
    return pl.pallas_call(
        _ln_kernel,
        grid=(m // tm,),
        in_specs=[
            pl.BlockSpec((tm, d), lambda i: (i, 0)),
            pl.BlockSpec((1, d), lambda i: (0, 0)),
            pl.BlockSpec((1, d), lambda i: (0, 0)),
        ],
        out_specs=[
            pl.BlockSpec((tm, d), lambda i: (i, 0)),
            pl.BlockSpec((tm, d), lambda i: (i, 0)),
        ],
        out_shape=[jax.ShapeDtypeStruct((m, d), F32), jax.ShapeDtypeStruct((m, d), BF16)],
        compiler_params=_cparams(1),
        name=name,
    )(y, gain.reshape(1, d), bias.reshape(1, d))


ATTN_QB = 4 * CHUNK
ATTN_WIN = ATTN_QB + LEFT_CHUNKS * CHUNK
ATTN_NVAR = LEFT_CHUNKS * CHUNK // ATTN_QB + 1


def _attn_bias_tiles(rel_bias):
    i = jnp.arange(ATTN_QB)[None, :, None]
    j = jnp.arange(ATTN_WIN)[None, None, :]
    off = (jnp.arange(ATTN_NVAR) * ATTN_QB)[:, None, None]
    qpos = off + i
    rel = qpos - j
    dchunk = qpos // CHUNK - j // CHUNK
    valid = (dchunk >= 0) & (dchunk <= LEFT_CHUNKS)
    idx = jnp.clip(rel, -MAX_REL, MAX_REL) + MAX_REL
    tiles = rel_bias.astype(F32)[:, idx]
    tiles = jnp.where(valid[None], tiles, NEG_BIG)
    return jnp.swapaxes(tiles, 0, 1)


def _attn_kernel(q_ref, k_ref, v_ref, bias_ref, o_ref, *, hb, seq):
    scale = DH_ATTN ** -0.5
    nq = seq // ATTN_QB
    for h in range(hb):
        cols = slice(h * DH_ATTN, (h + 1) * DH_ATTN)

        def body(qi, carry, cols=cols, h=h):
            qs = pl.multiple_of(qi * ATTN_QB, ATTN_QB)
            start = pl.multiple_of(jnp.maximum(qs - LEFT_CHUNKS * CHUNK, 0), ATTN_QB)
            q = q_ref[pl.ds(qs, ATTN_QB), cols]
            k = k_ref[pl.ds(start, ATTN_WIN), cols]
            v = v_ref[pl.ds(start, ATTN_WIN), cols]
            s = lax.dot_general(q, k, (((1,), (1,)), ((), ())), preferred_element_type=F32)
            var = jnp.minimum(qi, ATTN_NVAR - 1)
            s = s * scale + bias_ref[var, h]
            m = jnp.max(s, axis=-1, keepdims=True)
            p = jnp.exp(s - m)
            l = jnp.sum(p, axis=-1, keepdims=True)
            p = (p / l).astype(BF16)
            o = jnp.dot(p, v, preferred_element_type=F32)
            o_ref[pl.ds(qs, ATTN_QB), cols] = o.astype(o_ref.dtype)
            return carry

        lax.fori_loop(0, nq, body, 0)


def attention(qkv, rel_bias, n_heads, *, hb, name):
    b, seq, _ = qkv.shape
    assert n_heads % hb == 0 and seq % ATTN_QB == 0 and seq >= ATTN_WIN
    ng = n_heads // hb
    wcols = hb * DH_ATTN
    bias = _attn_bias_tiles(rel_bias)
    kern = functools.partial(_attn_kernel, hb=hb, seq=seq)
    return pl.pallas_call(
        kern,
        grid=(b, ng),
        in_specs=[
            pl.BlockSpec((None, seq, wcols), lambda bi, g: (bi, 0, g)),
            pl.BlockSpec((None, seq, wcols), lambda bi, g: (bi, 0, ng + g)),
            pl.BlockSpec((None, seq, wcols), lambda bi, g: (bi, 0, 2 * ng + g)),
            pl.BlockSpec((ATTN_NVAR, hb, ATTN_QB, ATTN_WIN), lambda bi, g: (0, g, 0, 0)),
        ],
        out_specs=pl.BlockSpec((None, seq, wcols), lambda bi, g: (bi, 0, g)),
        out_shape=jax.ShapeDtypeStruct((b, seq, n_heads * DH_ATTN), BF16),
        compiler_params=_cparams(2),
        name=name,
    )(qkv, qkv, qkv, bias)


def _ssm_tables(lam_re, lam_im, log_step, b_re, b_im, c_re, c_im, n_scan_steps):
    g, p = lam_re.shape
    hg = b_re.shape[-1]
    tc = SSM_BLOCK
    lam = lax.complex(jnp.minimum(lam_re.astype(F32), -1e-4), lam_im.astype(F32))
    step = jnp.exp(log_step.astype(F32))[:, None]
    lam_bar = jnp.exp(lam * step)
    b_bar = ((lam_bar - 1.0) / lam)[..., None] * lax.complex(b_re.astype(F32), b_im.astype(F32))
    pw = [jnp.ones_like(lam_bar)]
    for _ in range(tc):
        pw.append(pw[-1] * lam_bar)
    pw = jnp.stack(pw)
    xb = pw[:tc, :, :, None] * b_bar[None]
    xb_re, xb_im = jnp.real(xb), jnp.imag(xb)
    cr, ci = c_re.astype(F32), c_im.astype(F32)
    kmat = (jnp.einsum('ghp,tgpi->tghi', cr, xb_re, precision=HI)
            - jnp.einsum('ghp,tgpi->tghi', ci, xb_im, precision=HI))
    s_idx = jnp.arange(tc)[:, None]
    t_idx = jnp.arange(tc)[None, :]
    lag = t_idx - s_idx
    kst = kmat[jnp.clip(lag, 0, tc - 1)]
    kst = jnp.where((lag >= 0)[:, :, None, None, None], kst, 0.0)
    m_tab = jnp.transpose(kst, (2, 0, 4, 1, 3)).reshape(g, tc * hg, tc * hg)
    xr = jnp.transpose(xb_re[::-1], (1, 0, 3, 2)).reshape(g, tc * hg, p)
    xi = jnp.transpose(xb_im[::-1], (1, 0, 3, 2)).reshape(g, tc * hg, p)
    f_tab = jnp.concatenate([xr, xi], axis=-1)
    cc = lax.complex(cr, ci)
    gmat = cc[None] * pw[1:, :, None, :]
    e_re = jnp.transpose(jnp.real(gmat), (1, 3, 0, 2)).reshape(g, p, tc * hg)
    e_im = jnp.transpose(-jnp.imag(gmat), (1, 3, 0, 2)).reshape(g, p, tc * hg)
    e_tab = jnp.concatenate([e_re, e_im], axis=1)
    a = pw[tc]
    a1, a2 = [], []
    for _ in range(n_scan_steps):
        a1.append(jnp.concatenate([jnp.real(a), jnp.real(a)], axis=-1))
        a2.append(jnp.concatenate([-jnp.imag(a), jnp.imag(a)], axis=-1))
        a = a * a
    pad = (-n_scan_steps) % 8
    a1 = jnp.pad(jnp.stack(a1, axis=1), ((0, 0), (0, pad), (0, 0)))
    a2 = jnp.pad(jnp.stack(a2, axis=1), ((0, 0), (0, pad), (0, 0)))
    return m_tab, f_tab, e_tab, a1, a2


def _ssm_kernel(u_ref, m_ref, f_ref, e_ref, a1_ref, a2_ref, y_ref, *, gb, nb, nc, nsteps):
    p2 = 2 * SSM_STATE
    rows = lax.broadcasted_iota(jnp.int32, (nc, p2), 0)

    def per_group(g, carry):
        u = u_ref[g]
        sloc = jnp.dot(u, f_ref[g], preferred_element_type=F32)
        prevs = []
        for b in range(nb):
            x = sloc[b * nc:(b + 1) * nc]
            for k in range(nsteps):
                d = 1 << k
                sh = jnp.where(rows >= d, pltpu.roll(x, d, 0), 0.0)
                sw = pltpu.roll(sh, SSM_STATE, 1)
                x = x + a1_ref[g, k:k + 1, :] * sh + a2_ref[g, k:k + 1, :] * sw
            prevs.append(jnp.where(rows >= 1, pltpu.roll(x, 1, 0), 0.0))
        sprev = jnp.concatenate(prevs, axis=0) if nb > 1 else prevs[0]
        y = jnp.dot(u, m_ref[g], preferred_element_type=F32)
        y = y + jnp.dot(sprev.astype(BF16), e_ref[g], preferred_element_type=F32)
        y_ref[g] = y
        return carry

    lax.fori_loop(0, gb, per_group, 0)


def ssm(u, lam_re, lam_im, log_step, b_re, b_im, c_re, c_im, *, gb, name):
    b, seq, ds = u.shape
    g = lam_re.shape[0]
    hg = ds // g
    tc = SSM_BLOCK
    nc = seq // tc
    assert seq % tc == 0 and g % gb == 0 and nc & (nc - 1) == 0
    nsteps = nc.bit_length() - 1
    m_tab, f_tab, e_tab, a1, a2 = _ssm_tables(lam_re, lam_im, log_step, b_re, b_im, c_re, c_im, nsteps)
    ut = u.reshape(b, nc, tc, g, hg).transpose(3, 0, 1, 2, 4).reshape(g, b * nc, tc * hg).astype(BF16)
    kw = tc * hg
    p2 = 2 * SSM_STATE
    sp = a1.shape[1]
    kern = functools.partial(_ssm_kernel, gb=gb, nb=b, nc=nc, nsteps=nsteps)
    yt = pl.pallas_call(
        kern,
        grid=(g // gb,),
        in_specs=[
            pl.BlockSpec((gb, b * nc, kw), lambda i: (i, 0, 0)),
            pl.BlockSpec((gb, kw, kw), lambda i: (i, 0, 0)),
            pl.BlockSpec((gb, kw, p2), lambda i: (i, 0, 0)),
            pl.BlockSpec((gb, p2, kw), lambda i: (i, 0, 0)),
            pl.BlockSpec((gb, sp, p2), lambda i: (i, 0, 0)),
            pl.BlockSpec((gb, sp, p2), lambda i: (i, 0, 0)),
        ],
        out_specs=pl.BlockSpec((gb, b * nc, kw), lambda i: (i, 0, 0)),
        out_shape=jax.ShapeDtypeStruct((g, b * nc, kw), F32),
        compiler_params=_cparams(1),
        name=name,
    )(ut, m_tab.astype(BF16), f_tab.astype(BF16), e_tab.astype(BF16), a1, a2)
    return yt.reshape(g, b, nc, tc, hg).transpose(1, 2, 3, 0, 4).reshape(b, seq, ds)


def _gelu_tanh(x):
    c = math.sqrt(2.0 / math.pi)
    return 0.5 * x * (1.0 + jnp.tanh(c * (x + 0.044715 * (x * x * x))))


def _glu_kernel(y_ref, u_ref, d_ref, w_ref, b_ref, o_ref):
    yb = _gelu_tanh(y_ref[...] + d_ref[...] * u_ref[...])
    z = jnp.dot(yb.astype(BF16), w_ref[...], preferred_element_type=F32) + b_ref[...]
    o_ref[...] = (yb * _sigmoid(z)).astype(o_ref.dtype)


def ssm_glu(y, u, d_skip, w_glu, b_glu, *, tm, name):
    m, d = y.shape
    assert m % tm == 0
    return pl.pallas_call(
        _glu_kernel,
        grid=(m // tm,),
        in_specs=[
            pl.BlockSpec((tm, d), lambda i: (i, 0)),
            pl.BlockSpec((tm, d), lambda i: (i, 0)),
            pl.BlockSpec((1, d), lambda i: (0, 0)),
            pl.BlockSpec((d, d), lambda i: (0, 0)),
            pl.BlockSpec((1, d), lambda i: (0, 0)),
        ],
        out_specs=pl.BlockSpec((tm, d), lambda i: (i, 0)),
        out_shape=jax.ShapeDtypeStruct((m, d), BF16),
        compiler_params=_cparams(1),
        name=name,
    )(y, u, d_skip.reshape(1, d).astype(F32), w_glu, b_glu.reshape(1, d).astype(F32))


def _softplus(x):
    return jnp.maximum(x, 0.0) + jnp.log(1.0 + jnp.exp(-jnp.abs(x)))


def _cumsum_rows(x, n):
    rows = lax.broadcasted_iota(jnp.int32, x.shape, 0)
    d = 1
    while d < n:
        x = x + jnp.where(rows >= d, pltpu.roll(x, d, 0), 0.0)
        d *= 2
    return x


def _cumsum_lanes(x, n):
    cols = lax.broadcasted_iota(jnp.int32, x.shape, 1)
    d = 1
    while d < n:
        x = x + jnp.where(cols >= d, pltpu.roll(x, d, 1), 0.0)
        d *= 2
    return x


def _delta_kernel(q_ref, k_ref, v_ref, gate_ref, bgc_ref, bgr_ref, cwq_ref, cwk_ref, cwv_ref,
                  pc_ref, pr_ref, nw_ref, o_ref, s_ref, halo_ref, *, hb, tb):
    c = CHUNK
    dk = DK_DELTA
    w = hb * dk
    scale = dk ** -0.5
    n_chunks = tb // c
    ti = pl.program_id(2)

    @pl.when(ti == 0)
    def _():
        s_ref[...] = jnp.zeros_like(s_ref)
        halo_ref[...] = jnp.zeros_like(halo_ref)

    ri = lax.broadcasted_iota(jnp.int32, (c, c), 0)
    ci = lax.broadcasted_iota(jnp.int32, (c, c), 1)
    causal = ri >= ci
    strict = ri > ci
    alog_c = pc_ref[0:1, :]
    dtb_c = pc_ref[1:2, :]
    alog_r = pr_ref[:, 0:1]
    dtb_r = pr_ref[:, 1:2]
    nw = nw_ref[...]

    def conv_silu(win, cw_ref):
        acc = (cw_ref[3:4, :] * win[8:8 + c] + cw_ref[2:3, :] * win[7:7 + c]
               + cw_ref[1:2, :] * win[6:6 + c] + cw_ref[0:1, :] * win[5:5 + c])
        return acc * _sigmoid(acc)

    def chunk_body(ch, tails):
        tq, tk, tv = tails
        r0 = pl.multiple_of(ch * c, c)
        curq = q_ref[pl.ds(r0, c), :].astype(F32)
        curk = k_ref[pl.ds(r0, c), :].astype(F32)
        curv = v_ref[pl.ds(r0, c), :].astype(F32)
        qa = conv_silu(jnp.concatenate([tq, curq], axis=0), cwq_ref)
        ka = conv_silu(jnp.concatenate([tk, curk], axis=0), cwk_ref)
        va = conv_silu(jnp.concatenate([tv, curv], axis=0), cwv_ref)
        gate = gate_ref[pl.ds(r0, c), :].astype(F32)

        bg = bgc_ref[pl.ds(r0, c), :]
        beta_all = _sigmoid(bg)
        g_all = -jnp.exp(alog_c) * _softplus(bg + dtb_c)
        gc_all = _cumsum_rows(g_all, c)
        bgt = bgr_ref[ch]
        gt_all = -jnp.exp(alog_r) * _softplus(bgt + dtb_r)
        gt_pad = jnp.concatenate([gt_all, jnp.zeros((2 * hb, LANES - c), F32)], axis=1)
        gct_all = _cumsum_lanes(gt_pad, c)[:, :c]

        for h in range(hb):
            sl = slice(h * dk, (h + 1) * dk)
            q = qa[:, sl]
            k = ka[:, sl]
            v = va[:, sl]
            q = q * lax.rsqrt(jnp.sum(q * q, axis=-1, keepdims=True) + NORM_EPS)
            k = k * lax.rsqrt(jnp.sum(k * k, axis=-1, keepdims=True) + NORM_EPS)
            qs = q * scale
            beta = beta_all[:, h:h + 1]
            gc_col = gc_all[:, hb + h:hb + h + 1]
            gc_row = gct_all[hb + h:hb + h + 1, :]
            decay = jnp.where(causal, jnp.exp(gc_col - gc_row), 0.0)
            egc = jnp.exp(gc_col)
            kb = k * beta
            lhs = jnp.concatenate([kb, qs], axis=0).astype(BF16)
            kq = lax.dot_general(lhs, k.astype(BF16), (((1,), (1,)), ((), ())),
                                 preferred_element_type=F32)
            nm = jnp.where(strict, kq[:c] * decay, 0.0)
            attn = kq[c:] * decay
            x = jnp.concatenate([v * beta, kb * egc], axis=1)
            x = x - jnp.dot(nm, x, preferred_element_type=F32, precision=HI)
            pwr = nm
            span = 2
            while span < c:
                pwr = jnp.dot(pwr, pwr, preferred_element_type=F32, precision=HI)
                x = x + jnp.dot(pwr, x, preferred_element_type=F32, precision=HI)
                span *= 2
            u_c = x[:, :dk]
            w_c = x[:, dk:]
            state = s_ref[h]
            sb = state.astype(BF16)
            lhs2 = jnp.concatenate([w_c, qs * egc], axis=0).astype(BF16)
            ws = jnp.dot(lhs2, sb, preferred_element_type=F32)
            v_new = u_c - ws[:c]
            vb = v_new.astype(BF16)
            out = ws[c:] + jnp.dot(attn.astype(BF16), vb, preferred_element_type=F32)
            g_last = gc_col[c - 1:c, :]
            kd = (k * jnp.exp(g_last - gc_col)).astype(BF16)
            upd = lax.dot_general(kd, vb, (((0,), (0,)), ((), ())), preferred_element_type=F32)
            s_ref[h] = state * jnp.exp(g_last) + upd
            o = out * lax.rsqrt(jnp.mean(out * out, axis=-1, keepdims=True) + NORM_EPS) * nw
            gt = gate[:, sl]
            o_ref[pl.ds(r0, c), sl] = (o * (gt * _sigmoid(gt))).astype(o_ref.dtype)

        return (curq[c - 8:], curk[c - 8:], curv[c - 8:])

    tails0 = (halo_ref[0], halo_ref[1], halo_ref[2])
    tq, tk, tv = lax.fori_loop(0, n_chunks, chunk_body, tails0)
    halo_ref[0] = tq
    halo_ref[1] = tk
    halo_ref[2] = tv


def gated_deltanet(proj, braw, conv_w, a_log, dt_bias, norm_w, n_heads, *, hb, tb, name):
    b, seq, _ = proj.shape
    dk = DK_DELTA
    c = CHUNK
    assert n_heads % hb == 0 and seq % tb == 0 and tb % c == 0
    ng = n_heads // hb
    w = hb * dk
    nct = seq // c
    br = braw.reshape(b, seq, 2, ng, hb)
    bgc = jnp.transpose(br, (0, 3, 1, 2, 4)).reshape(b, ng, seq, 2 * hb)
    bgr = jnp.transpose(bgc.reshape(b, ng, nct, c, 2 * hb), (0, 1, 2, 4, 3))
    zeros = jnp.zeros((ng, hb), F32)
    al = jnp.concatenate([zeros, a_log.astype(F32).reshape(ng, hb)], axis=1)
    dt = jnp.concatenate([zeros, dt_bias.astype(F32).reshape(ng, hb)], axis=1)
    pc = jnp.stack([al, dt], axis=1)
    pr = jnp.stack([al, dt], axis=2)
    cw = conv_w.astype(F32)
    kern = functools.partial(_delta_kernel, hb=hb, tb=tb)
    return pl.pallas_call(
        kern,
        grid=(b, ng, seq // tb),
        in_specs=[
            pl.BlockSpec((None, tb, w), lambda bi, g, t: (bi, t, g)),
            pl.BlockSpec((None, tb, w), lambda bi, g, t: (bi, t, ng + g)),
            pl.BlockSpec((None, tb, w), lambda bi, g, t: (bi, t, 2 * ng + g)),
            pl.BlockSpec((None, tb, w), lambda bi, g, t: (bi, t, 3 * ng + g)),
            pl.BlockSpec((None, None, tb, 2 * hb), lambda bi, g, t: (bi, g, t, 0)),
            pl.BlockSpec((None, None, tb // c, 2 * hb, c), lambda bi, g, t: (bi, g, t, 0, 0)),
            pl.BlockSpec((CONV_K, w), lambda bi, g, t: (0, g)),
            pl.BlockSpec((CONV_K, w), lambda bi, g, t: (0, ng + g)),
            pl.BlockSpec((CONV_K, w), lambda bi, g, t: (0, 2 * ng + g)),
            pl.BlockSpec((None, 2, 2 * hb), lambda bi, g, t: (g, 0, 0)),
            pl.BlockSpec((None, 2 * hb, 2), lambda bi, g, t: (g, 0, 0)),
            pl.BlockSpec((1, dk), lambda bi, g, t: (0, 0)),
        ],
        out_specs=pl.BlockSpec((None, tb, w), lambda bi, g, t: (bi, t, g)),
        out_shape=jax.ShapeDtypeStruct((b, seq, n_heads * dk), BF16),
        scratch_shapes=[pltpu.VMEM((hb, dk, dk), F32), pltpu.VMEM((3, 8, w), F32)],
        compiler_params=_cparams(3),
        name=name,
    )(proj, proj, proj, proj, bgc, bgr, cw, cw, cw, pc, pr, norm_w.astype(F32).reshape(1, dk))


def _ffn(x32, x16, w_gate, w_up, w_down, gain, bias, layer):
    d, dff = w_gate.shape
    pad = (-dff) % 1024
    wg = jnp.pad(w_gate, ((0, 0), (0, pad))).astype(BF16)
    wu = jnp.pad(w_up, ((0, 0), (0, pad))).astype(BF16)
    wd = jnp.pad(w_down, ((0, pad), (0, 0))).astype(BF16)
    h = swiglu_up(x16, wg, wu, tm=1024, tn=512, name=f"ffn_up_{layer}")
    y = matmul_resid(h, wd, x32, ALPHA, tm=1024, tn=1024, tk=1024, name=f"ffn_down_{layer}")
    return layer_norm(y, gain, bias, tm=256, name=f"ffn_ln_{layer}")


def kernel(x, ab_w_in, ab_rel_bias, ab_lam_re, ab_lam_im, ab_log_step, ab_b_re, ab_b_im, ab_c_re, ab_c_im, ab_d_skip, ab_w_glu, ab_b_glu, ab_w_out, c_w_in, c_conv, c_a_log, c_dt_bias, c_norm_w, c_w_out, ffn_w_gate, ffn_w_up, ffn_w_down, ln_gain, ln_bias):
    b, seq, d = x.shape
    t = b * seq
    x32 = x.reshape(t, d).astype(F32)
    x16 = x32.astype(BF16)

    n_heads_a = ab_rel_bias.shape[1]
    d_attn = n_heads_a * DH_ATTN
    w_in = ab_w_in[0].astype(BF16)
    qkv = matmul(x16, w_in[:, :3 * d_attn], BF16, tm=1024, tn=512, name="ab_in_qkv")
    u = matmul(x16, w_in[:, 3 * d_attn:], F32, tm=1024, tn=512, name="ab_in_u")
    d_ssm = u.shape[1]
    y_a = attention(qkv.reshape(b, seq, 3 * d_attn), ab_rel_bias[0], n_heads_a, hb=2, name="ab_attn")
    y_s = ssm(u.reshape(b, seq, d_ssm), ab_lam_re[0], ab_lam_im[0], ab_log_step[0], ab_b_re[0],
              ab_b_im[0], ab_c_re[0], ab_c_im[0], gb=8, name="ab_ssm")
    y_b = ssm_glu(y_s.reshape(t, d_ssm), u, ab_d_skip[0].reshape(-1), ab_w_glu[0].astype(BF16),
                  ab_b_glu[0], tm=512, name="ab_glu")
    y_ab = jnp.concatenate([y_a.reshape(t, d_attn), y_b], axis=-1)
    y = matmul_resid(y_ab, ab_w_out[0].astype(BF16), x32, ALPHA, tm=1024, tn=1024, tk=1024,
                     name="ab_out")
    x32, x16 = layer_norm(y, ln_gain[0, 0], ln_bias[0, 0], tm=256, name="ab_ln")
    x32, x16 = _ffn(x32, x16, ffn_w_gate[0], ffn_w_up[0], ffn_w_down[0], ln_gain[0, 1], ln_bias[0, 1], 0)

    n_heads_c = c_a_log.shape[1]
    d_delta = n_heads_c * DK_DELTA
    cw_in = c_w_in[0]
    proj = matmul(x16, cw_in[:, :4 * d_delta].astype(BF16), BF16, tm=1024, tn=512, name="c_in")
    braw = matmul(x16, cw_in[:, 4 * d_delta:].astype(BF16), F32, tm=1024, tn=2 * n_heads_c,
                  name="c_in_small")
    o = gated_deltanet(proj.reshape(b, seq, 4 * d_delta), braw.reshape(b, seq, 2 * n_heads_c),
                       c_conv[0], c_a_log[0], c_dt_bias[0], c_norm_w[0], n_heads_c,
                       hb=4, tb=1024, name="c_delta")
    y = matmul_resid(o.reshape(t, d_delta), c_w_out[0].astype(BF16), x32, ALPHA, tm=1024, tn=1024,
                     tk=1024, name="c_out")
    x32, x16 = layer_norm(y, ln_gain[1, 0], ln_bias[1, 0], tm=256, name="c_ln")
    x32, x16 = _ffn(x32, x16, ffn_w_gate[1], ffn_w_up[1], ffn_w_down[1], ln_gain[1, 1], ln_bias[1, 1], 1)
    return x32.reshape(b, seq, d).astype(x.dtype)
```

```python
import functools
import math

import jax
import jax.numpy as jnp
from jax import lax
from jax.experimental import pallas as pl
from jax.experimental.pallas import tpu as pltpu

F32 = jnp.float32
BF16 = jnp.bfloat16

CHUNK = 64
LEFT_CHUNKS = 8
DH_ATTN = 128
MAX_REL = 256
SSM_GROUP = 16
SSM_STATE = 64
DK_DELTA = 128
CONV_K = 4
DEPTH = 2
ALPHA = (2.0 * DEPTH) ** 0.25
LN_EPS = 1e-5
NORM_EPS = 1e-6
NEG_BIG = -1e30

V7X_VMEM_LIMIT_BYTES = 56 * 1024 * 1024
LANES = 128
SSM_BLOCK = 16
HI = lax.Precision.HIGHEST


def _cparams(n_axes, flags=None):
    return pltpu.CompilerParams(
        dimension_semantics=("arbitrary",) * n_axes,
        vmem_limit_bytes=V7X_VMEM_LIMIT_BYTES,
        flags=flags,
    )


def _sigmoid(x):
    return 1.0 / (1.0 + jnp.exp(-x))


def _mm_kernel(x_ref, w_ref, o_ref):
    o_ref[...] = jnp.dot(x_ref[...], w_ref[...], preferred_element_type=F32).astype(o_ref.dtype)


def matmul(x, w, out_dtype, *, tm, tn, name):
    m, k = x.shape
    n = w.shape[1]
    assert m % tm == 0 and n % tn == 0
    return pl.pallas_call(
        _mm_kernel,
        grid=(m // tm, n // tn),
        in_specs=[
            pl.BlockSpec((tm, k), lambda i, j: (i, 0)),
            pl.BlockSpec((k, tn), lambda i, j: (0, j)),
        ],
        out_specs=pl.BlockSpec((tm, tn), lambda i, j: (i, j)),
        out_shape=jax.ShapeDtypeStruct((m, n), out_dtype),
        compiler_params=_cparams(2),
        name=name,
    )(x, w)


def _mm_resid_kernel(x_ref, w_ref, r_ref, o_ref, acc_ref, *, alpha, nk):
    kk = pl.program_id(2)

    @pl.when(kk == 0)
    def _():
        acc_ref[...] = alpha * r_ref[...]

    acc_ref[...] += jnp.dot(x_ref[...], w_ref[...], preferred_element_type=F32)

    @pl.when(kk == nk - 1)
    def _():
        o_ref[...] = acc_ref[...]


def matmul_resid(x, w, resid, alpha, *, tm, tn, tk, name):
    m, k = x.shape
    n = w.shape[1]
    assert m % tm == 0 and n % tn == 0 and k % tk == 0
    nk = k // tk
    return pl.pallas_call(
        functools.partial(_mm_resid_kernel, alpha=alpha, nk=nk),
        grid=(m // tm, n // tn, nk),
        in_specs=[
            pl.BlockSpec((tm, tk), lambda i, j, kk: (i, kk)),
            pl.BlockSpec((tk, tn), lambda i, j, kk: (kk, j)),
            pl.BlockSpec((tm, tn), lambda i, j, kk: (i, j)),
        ],
        out_specs=pl.BlockSpec((tm, tn), lambda i, j, kk: (i, j)),
        out_shape=jax.ShapeDtypeStruct((m, n), F32),
        scratch_shapes=[pltpu.VMEM((tm, tn), F32)],
        compiler_params=_cparams(3),
        name=name,
    )(x, w, resid)


def _swiglu_kernel(x_ref, wg_ref, wu_ref, o_ref):
    x = x_ref[...]
    g = jnp.dot(x, wg_ref[...], preferred_element_type=F32)
    u = jnp.dot(x, wu_ref[...], preferred_element_type=F32)
    o_ref[...] = (g * _sigmoid(g) * u).astype(o_ref.dtype)


def swiglu_up(x, wg, wu, *, tm, tn, name):
    m, k = x.shape
    n = wg.shape[1]
    assert m % tm == 0 and n % tn == 0
    return pl.pallas_call(
        _swiglu_kernel,
        grid=(m // tm, n // tn),
        in_specs=[
            pl.BlockSpec((tm, k), lambda i, j: (i, 0)),
            pl.BlockSpec((k, tn), lambda i, j: (0, j)),
            pl.BlockSpec((k, tn), lambda i, j: (0, j)),
        ],
        out_specs=pl.BlockSpec((tm, tn), lambda i, j: (i, j)),
        out_shape=jax.ShapeDtypeStruct((m, n), BF16),
        compiler_params=_cparams(2),
        name=name,
    )(x, wg, wu)


def _ln_kernel(y_ref, g_ref, b_ref, o32_ref, o16_ref):
    y = y_ref[...]
    mu = jnp.mean(y, axis=-1, keepdims=True)
    yc = y - mu
    var = jnp.mean(yc * yc, axis=-1, keepdims=True)
    x = yc * lax.rsqrt(var + LN_EPS) * g_ref[...] + b_ref[...]
    o32_ref[...] = x
    o16_ref[...] = x.astype(BF16)


def layer_norm(y, gain, bias, *, tm, name):
    m, d = y.shape
    assert m % tm == 0
    return pl.pallas_call(
        _ln_kernel,
        grid=(m // tm,),
        in_specs=[
            pl.BlockSpec((tm, d), lambda i: (i, 0)),
            pl.BlockSpec((1, d), lambda i: (0, 0)),
            pl.BlockSpec((1, d), lambda i: (0, 0)),
        ],
        out_specs=[
            pl.BlockSpec((tm, d), lambda i: (i, 0)),
            pl.BlockSpec((tm, d), lambda i: (i, 0)),
        ],
        out_shape=[jax.ShapeDtypeStruct((m, d), F32), jax.ShapeDtypeStruct((m, d), BF16)],
        compiler_params=_cparams(1),
        name=name,
    )(y, gain.reshape(1, d), bias.reshape(1, d))


ATTN_QB = 4 * CHUNK
ATTN_WIN = ATTN_QB + LEFT_CHUNKS * CHUNK
ATTN_NVAR = LEFT_CHUNKS * CHUNK // ATTN_QB + 1


def _attn_bias_tiles(rel_bias):
    n_heads = rel_bias.shape[0]
    span = ATTN_QB + ATTN_WIN - 1
    period = span + 1
    m = jnp.arange(period)
    tiles = []
    for v in range(ATTN_NVAR):
        rel = v * ATTN_QB + (ATTN_QB - 1) - m
        hank = rel_bias.astype(F32)[:, jnp.clip(rel, -MAX_REL, MAX_REL) + MAX_REL]
        flat = jnp.tile(hank, (1, ATTN_QB + 2))[:, :ATTN_QB * (period + 1)]
        skew = flat.reshape(n_heads, ATTN_QB, period + 1)[:, ::-1, :ATTN_WIN]
        tiles.append(skew)
    tiles = jnp.stack(tiles)
    i = jnp.arange(ATTN_QB)[None, :, None]
    j = jnp.arange(ATTN_WIN)[None, None, :]
    qpos = (jnp.arange(ATTN_NVAR) * ATTN_QB)[:, None, None] + i
    dchunk = qpos // CHUNK - j // CHUNK
    valid = (dchunk >= 0) & (dchunk <= LEFT_CHUNKS)
    return jnp.where(valid[:, None], tiles, NEG_BIG)


def _attn_kernel(q_ref, k_ref, v_ref, bias_ref, o_ref, *, hb, seq):
    scale = DH_ATTN ** -0.5
    nq = seq // ATTN_QB
    for h in range(hb):
        cols = slice(h * DH_ATTN, (h + 1) * DH_ATTN)

        def body(qi, carry, cols=cols, h=h):
            qs = pl.multiple_of(qi * ATTN_QB, ATTN_QB)
            start = pl.multiple_of(jnp.maximum(qs - LEFT_CHUNKS * CHUNK, 0), ATTN_QB)
            q = q_ref[pl.ds(qs, ATTN_QB), cols]
            k = k_ref[pl.ds(start, ATTN_WIN), cols]
            v = v_ref[pl.ds(start, ATTN_WIN), cols]
            s = lax.dot_general(q, k, (((1,), (1,)), ((), ())), preferred_element_type=F32)
            var = jnp.minimum(qi, ATTN_NVAR - 1)
            s = s * scale + bias_ref[var, h]
            m = jnp.max(s, axis=-1, keepdims=True)
            p = jnp.exp(s - m)
            l = jnp.sum(p, axis=-1, keepdims=True)
            p = (p / l).astype(BF16)
            o = jnp.dot(p, v, preferred_element_type=F32)
            o_ref[pl.ds(qs, ATTN_QB), cols] = o.astype(o_ref.dtype)
            return carry

        lax.fori_loop(0, nq, body, 0)


def attention(qkv, rel_bias, n_heads, *, hb, name):
    b, seq, _ = qkv.shape
    assert n_heads % hb == 0 and seq % ATTN_QB == 0 and seq >= ATTN_WIN
    ng = n_heads // hb
    wcols = hb * DH_ATTN
    bias = _attn_bias_tiles(rel_bias)
    kern = functools.partial(_attn_kernel, hb=hb, seq=seq)
    return pl.pallas_call(
        kern,
        grid=(b, ng),
        in_specs=[
            pl.BlockSpec((None, seq, wcols), lambda bi, g: (bi, 0, g)),
            pl.BlockSpec((None, seq, wcols), lambda bi, g: (bi, 0, ng + g)),
            pl.BlockSpec((None, seq, wcols), lambda bi, g: (bi, 0, 2 * ng + g)),
            pl.BlockSpec((ATTN_NVAR, hb, ATTN_QB, ATTN_WIN), lambda bi, g: (0, g, 0, 0)),
        ],
        out_specs=pl.BlockSpec((None, seq, wcols), lambda bi, g: (bi, 0, g)),
        out_shape=jax.ShapeDtypeStruct((b, seq, n_heads * DH_ATTN), BF16),
        compiler_params=_cparams(2),
        name=name,
    )(qkv, qkv, qkv, bias)


def _ssm_tables(lam_re, lam_im, log_step, b_re, b_im, c_re, c_im, n_scan_steps):
    g, p = lam_re.shape
    hg = b_re.shape[-1]
    tc = SSM_BLOCK
    lam = lax.complex(jnp.minimum(lam_re.astype(F32), -1e-4), lam_im.astype(F32))
    step = jnp.exp(log_step.astype(F32))[:, None]
    lam_bar = jnp.exp(lam * step)
    b_bar = ((lam_bar - 1.0) / lam)[..., None] * lax.complex(b_re.astype(F32), b_im.astype(F32))
    pw = [jnp.ones_like(lam_bar)]
    for _ in range(tc):
        pw.append(pw[-1] * lam_bar)
    pw = jnp.stack(pw)
    xb = pw[:tc, :, :, None] * b_bar[None]
    xb_re, xb_im = jnp.real(xb), jnp.imag(xb)
    cr, ci = c_re.astype(F32), c_im.astype(F32)
    kmat = (jnp.einsum('ghp,tgpi->tghi', cr, xb_re, precision=HI)
            - jnp.einsum('ghp,tgpi->tghi', ci, xb_im, precision=HI))
    s_idx = jnp.arange(tc)[:, None]
    t_idx = jnp.arange(tc)[None, :]
    lag = t_idx - s_idx
    kst = kmat[jnp.clip(lag, 0, tc - 1)]
    kst = jnp.where((lag >= 0)[:, :, None, None, None], kst, 0.0)
    m_tab = jnp.transpose(kst, (2, 0, 4, 1, 3)).reshape(g, tc * hg, tc * hg)
    xr = jnp.transpose(xb_re[::-1], (1, 0, 3, 2)).reshape(g, tc * hg, p)
    xi = jnp.transpose(xb_im[::-1], (1, 0, 3, 2)).reshape(g, tc * hg, p)
    f_tab = jnp.concatenate([xr, xi], axis=-1)
    cc = lax.complex(cr, ci)
    gmat = cc[None] * pw[1:, :, None, :]
    e_re = jnp.transpose(jnp.real(gmat), (1, 3, 0, 2)).reshape(g, p, tc * hg)
    e_im = jnp.transpose(-jnp.imag(gmat), (1, 3, 0, 2)).reshape(g, p, tc * hg)
    e_tab = jnp.concatenate([e_re, e_im], axis=1)
    a = pw[tc]
    a1, a2 = [], []
    for _ in range(n_scan_steps):
        a1.append(jnp.concatenate([jnp.real(a), jnp.real(a)], axis=-1))
        a2.append(jnp.concatenate([-jnp.imag(a), jnp.imag(a)], axis=-1))
        a = a * a
    pad = (-n_scan_steps) % 8
    a1 = jnp.pad(jnp.stack(a1, axis=1), ((0, 0), (0, pad), (0, 0)))
    a2 = jnp.pad(jnp.stack(a2, axis=1), ((0, 0), (0, pad), (0, 0)))
    return m_tab, f_tab, e_tab, a1, a2


def _ssm_kernel(u_ref, m_ref, f_ref, e_ref, a1_ref, a2_ref, y_ref, *, gb, nb, nc, nsteps):
    p2 = 2 * SSM_STATE
    rows = lax.broadcasted_iota(jnp.int32, (nc, p2), 0)

    def per_group(g, carry):
        u = u_ref[g]
        sloc = jnp.dot(u, f_ref[g], preferred_element_type=F32)
        prevs = []
        for b in range(nb):
            x = sloc[b * nc:(b + 1) * nc]
            for k in range(nsteps):
                d = 1 << k
                sh = jnp.where(rows >= d, pltpu.roll(x, d, 0), 0.0)
                sw = pltpu.roll(sh, SSM_STATE, 1)
                x = x + a1_ref[g, k:k + 1, :] * sh + a2_ref[g, k:k + 1, :] * sw
            prevs.append(jnp.where(rows >= 1, pltpu.roll(x, 1, 0), 0.0))
        sprev = jnp.concatenate(prevs, axis=0) if nb > 1 else prevs[0]
        y = jnp.dot(u, m_ref[g], preferred_element_type=F32)
        y = y + jnp.dot(sprev.astype(BF16), e_ref[g], preferred_element_type=F32)
        y_ref[g] = y
        return carry

    lax.fori_loop(0, gb, per_group, 0)


def ssm(u, lam_re, lam_im, log_step, b_re, b_im, c_re, c_im, *, gb, name):
    b, seq, ds = u.shape
    g = lam_re.shape[0]
    hg = ds // g
    tc = SSM_BLOCK
    nc = seq // tc
    assert seq % tc == 0 and g % gb == 0 and nc & (nc - 1) == 0
    nsteps = nc.bit_length() - 1
    m_tab, f_tab, e_tab, a1, a2 = _ssm_tables(lam_re, lam_im, log_step, b_re, b_im, c_re, c_im, nsteps)
    ut = u.reshape(b, nc, tc, g, hg).transpose(3, 0, 1, 2, 4).reshape(g, b * nc, tc * hg).astype(BF16)
    kw = tc * hg
    p2 = 2 * SSM_STATE
    sp = a1.shape[1]
    kern = functools.partial(_ssm_kernel, gb=gb, nb=b, nc=nc, nsteps=nsteps)
    yt = pl.pallas_call(
        kern,
        grid=(g // gb,),
        in_specs=[
            pl.BlockSpec((gb, b * nc, kw), lambda i: (i, 0, 0)),
            pl.BlockSpec((gb, kw, kw), lambda i: (i, 0, 0)),
            pl.BlockSpec((gb, kw, p2), lambda i: (i, 0, 0)),
            pl.BlockSpec((gb, p2, kw), lambda i: (i, 0, 0)),
            pl.BlockSpec((gb, sp, p2), lambda i: (i, 0, 0)),
            pl.BlockSpec((gb, sp, p2), lambda i: (i, 0, 0)),
        ],
        out_specs=pl.BlockSpec((gb, b * nc, kw), lambda i: (i, 0, 0)),
        out_shape=jax.ShapeDtypeStruct((g, b * nc, kw), F32),
        compiler_params=_cparams(1),
        name=name,
    )(ut, m_tab.astype(BF16), f_tab.astype(BF16), e_tab.astype(BF16), a1, a2)
    return yt.reshape(g, b, nc, tc, hg).transpose(1, 2, 3, 0, 4).reshape(b, seq, ds)


def _gelu_tanh(x):
    c = math.sqrt(2.0 / math.pi)
    return 0.5 * x * (1.0 + jnp.tanh(c * (x + 0.044715 * (x * x * x))))


def _glu_kernel(y_ref, u_ref, d_ref, w_ref, b_ref, o_ref):
    yb = _gelu_tanh(y_ref[...] + d_ref[...] * u_ref[...])
    z = jnp.dot(yb.astype(BF16), w_ref[...], preferred_element_type=F32) + b_ref[...]
    o_ref[...] = (yb * _sigmoid(z)).astype(o_ref.dtype)


def ssm_glu(y, u, d_skip, w_glu, b_glu, *, tm, name):
    m, d = y.shape
    assert m % tm == 0
    return pl.pallas_call(
        _glu_kernel,
        grid=(m // tm,),
        in_specs=[
            pl.BlockSpec((tm, d), lambda i: (i, 0)),
            pl.BlockSpec((tm, d), lambda i: (i, 0)),
            pl.BlockSpec((1, d), lambda i: (0, 0)),
            pl.BlockSpec((d, d), lambda i: (0, 0)),
            pl.BlockSpec((1, d), lambda i: (0, 0)),
        ],
        out_specs=pl.BlockSpec((tm, d), lambda i: (i, 0)),
        out_shape=jax.ShapeDtypeStruct((m, d), BF16),
        compiler_params=_cparams(1),
        name=name,
    )(y, u, d_skip.reshape(1, d).astype(F32), w_glu, b_glu.reshape(1, d).astype(F32))


def _softplus(x):
    return jnp.maximum(x, 0.0) + jnp.log(1.0 + jnp.exp(-jnp.abs(x)))


def _cumsum_rows(x, n):
    rows = lax.broadcasted_iota(jnp.int32, x.shape, 0)
    d = 1
    while d < n:
        x = x + jnp.where(rows >= d, pltpu.roll(x, d, 0), 0.0)
        d *= 2
    return x


def _cumsum_lane_segments(x, seg):
    pos = lax.broadcasted_iota(jnp.int32, x.shape, 1) & (seg - 1)
    d = 1
    while d < seg:
        x = x + jnp.where(pos >= d, pltpu.roll(x, d, 1), 0.0)
        d *= 2
    return x


DELTA_PACK = 4
DELTA_ROWS = DELTA_PACK * CHUNK
DELTA_LEVELS = CHUNK.bit_length() - 1


def _delta_level_map():
    import numpy as np
    i = np.arange(DELTA_ROWS)[:, None]
    j = np.arange(DELTA_ROWS)[None, :]
    x = i ^ j
    lev = np.zeros_like(x)
    for a in range(1, 8):
        lev += (x >> a) != 0
    out = np.where((i > j) & ((i // CHUNK) == (j // CHUNK)), lev, 99)
    out = np.where(i == j, -1, out)
    return out.astype(np.int32)


def _delta_kernel(q_ref, k_ref, v_ref, gate_ref, bgc_ref, gr_ref, cwq_ref, cwk_ref, cwv_ref,
                  pc_ref, pr_ref, nw_ref, lev_ref, o_ref, s_ref, halo_ref, *, hb, tb):
    c = CHUNK
    dk = DK_DELTA
    pk = DELTA_PACK
    rows = DELTA_ROWS
    sub = hb // pk
    scale = dk ** -0.5
    n_chunks = tb // c
    ti = pl.program_id(2)

    @pl.when(ti == 0)
    def _():
        s_ref[...] = jnp.zeros_like(s_ref)
        halo_ref[...] = jnp.zeros_like(halo_ref)

    alog_c = pc_ref[0:1, :]
    dtb_c = pc_ref[1:2, :]
    alog_r = pr_ref[0:sub, :]
    dtb_r = pr_ref[sub:2 * sub, :]
    nw = nw_ref[...]

    def conv_silu(win, cw_ref):
        acc = (cw_ref[3:4, :] * win[8:8 + c] + cw_ref[2:3, :] * win[7:7 + c]
               + cw_ref[1:2, :] * win[6:6 + c] + cw_ref[0:1, :] * win[5:5 + c])
        return acc * _sigmoid(acc)

    def stack_heads(x, s):
        return jnp.concatenate(
            [x[:, (pk * s + j) * dk:(pk * s + j + 1) * dk] for j in range(pk)], axis=0)

    def stack_cols(x, s, off):
        return jnp.concatenate(
            [x[:, off + pk * s + j:off + pk * s + j + 1] for j in range(pk)], axis=0)

    def chunk_body(ch, tails):
        tq, tk, tv = tails
        r0 = pl.multiple_of(ch * c, c)
        curq = q_ref[pl.ds(r0, c), :].astype(F32)
        curk = k_ref[pl.ds(r0, c), :].astype(F32)
        curv = v_ref[pl.ds(r0, c), :].astype(F32)
        qa = conv_silu(jnp.concatenate([tq, curq], axis=0), cwq_ref)
        ka = conv_silu(jnp.concatenate([tk, curk], axis=0), cwk_ref)
        va = conv_silu(jnp.concatenate([tv, curv], axis=0), cwv_ref)
        gate = gate_ref[pl.ds(r0, c), :].astype(F32)

        bg = bgc_ref[pl.ds(r0, c), :]
        beta_all = _sigmoid(bg)
        g_all = -jnp.exp(alog_c) * _softplus(bg + dtb_c)
        gc_all = _cumsum_rows(g_all, c)
        gl_all = jnp.broadcast_to(gc_all[c - 1:c, :], gc_all.shape)
        g_row = -jnp.exp(alog_r) * _softplus(gr_ref[ch] + dtb_r)
        gcr_all = _cumsum_lane_segments(g_row, c)

        lev = lev_ref[...]
        subs = []
        for s in range(sub):
            q = stack_heads(qa, s)
            k = stack_heads(ka, s)
            v = stack_heads(va, s)
            q = q * lax.rsqrt(jnp.sum(q * q, axis=-1, keepdims=True) + NORM_EPS)
            k = k * lax.rsqrt(jnp.sum(k * k, axis=-1, keepdims=True) + NORM_EPS)
            qs = q * scale
            beta = stack_cols(beta_all, s, 0)
            gc_col = stack_cols(gc_all, s, hb)
            gl_col = stack_cols(gl_all, s, hb)
            gc_row = gcr_all[s:s + 1, :]
            egc = jnp.exp(gc_col)
            kb = k * beta
            lhs = jnp.concatenate([kb, qs], axis=0).astype(BF16)
            kq = lax.dot_general(lhs, k.astype(BF16), (((1,), (1,)), ((), ())),
                                 preferred_element_type=F32)
            dec = jnp.exp(gc_col - gc_row)
            nm = kq[:rows] * dec
            subs.append(dict(
                nm=nm, qg=qs * egc,
                attn=jnp.where(lev < DELTA_LEVELS, kq[rows:] * dec, 0.0),
                x=jnp.concatenate([v * beta, kb * egc], axis=1),
                kd=(k * jnp.exp(gl_col - gc_col)).astype(BF16),
                t=jnp.where(lev == 0, -nm, jnp.where(lev == -1, 1.0, 0.0))))
        for a in range(1, DELTA_LEVELS):
            for st in subs:
                cm = jnp.where(lev == a, st["nm"], 0.0).astype(BF16)
                tb16 = st["t"].astype(BF16)
                pm = jnp.dot(tb16, cm, preferred_element_type=F32)
                st["t"] = st["t"] - jnp.dot(pm.astype(BF16), tb16, preferred_element_type=F32)
        for st in subs:
            t_off = jnp.where(lev == -1, 0.0, st["t"]).astype(BF16)
            x = st["x"]
            st["x"] = x + jnp.dot(t_off, x.astype(BF16), preferred_element_type=F32)
        for s, st in enumerate(subs):
            u_c = st["x"][:, :dk]
            w_c = st["x"][:, dk:]
            qg = st["qg"]
            ws_parts, qs_parts = [], []
            for j in range(pk):
                h = pk * s + j
                rs = slice(j * c, (j + 1) * c)
                lhs2 = jnp.concatenate([w_c[rs], qg[rs]], axis=0).astype(BF16)
                ws = jnp.dot(lhs2, s_ref[h].astype(BF16), preferred_element_type=F32)
                ws_parts.append(ws[:c])
                qs_parts.append(ws[c:])
            v_new = u_c - jnp.concatenate(ws_parts, axis=0)
            out = jnp.concatenate(qs_parts, axis=0) + jnp.dot(
                st["attn"].astype(BF16), v_new.astype(BF16), preferred_element_type=F32)
            ri = lax.broadcasted_iota(jnp.int32, (rows, pk * dk), 0)
            cj = lax.broadcasted_iota(jnp.int32, (rows, pk * dk), 1)
            vbd = jnp.where((ri // c) == (cj // dk), jnp.concatenate([v_new] * pk, axis=1), 0.0)
            upd = lax.dot_general(st["kd"], vbd.astype(BF16), (((0,), (0,)), ((), ())),
                                  preferred_element_type=F32)
            o = out * lax.rsqrt(jnp.mean(out * out, axis=-1, keepdims=True) + NORM_EPS) * nw
            gt = stack_heads(gate, s)
            res = (o * (gt * _sigmoid(gt))).astype(o_ref.dtype)
            for j in range(pk):
                h = pk * s + j
                e_last = jnp.exp(gc_all[c - 1:c, hb + h:hb + h + 1])
                s_ref[h] = s_ref[h] * e_last + upd[:, j * dk:(j + 1) * dk]
                o_ref[pl.ds(r0, c), h * dk:(h + 1) * dk] = res[j * c:(j + 1) * c]

        return (curq[c - 8:], curk[c - 8:], curv[c - 8:])

    tails0 = (halo_ref[0], halo_ref[1], halo_ref[2])
    tq, tk, tv = lax.fori_loop(0, n_chunks, chunk_body, tails0)
    halo_ref[0] = tq
    halo_ref[1] = tk
    halo_ref[2] = tv


def gated_deltanet(proj, braw, conv_w, a_log, dt_bias, norm_w, n_heads, *, hb, tb, name):
    b, seq, _ = proj.shape
    dk = DK_DELTA
    c = CHUNK
    pk = DELTA_PACK
    assert n_heads % hb == 0 and hb % pk == 0 and seq % tb == 0 and tb % c == 0
    ng = n_heads // hb
    sub = hb // pk
    w = hb * dk
    nct = seq // c
    br = braw.reshape(b, seq, 2, ng, hb)
    bgc = jnp.transpose(br, (0, 3, 1, 2, 4)).reshape(b, ng, seq, 2 * hb)
    araw = braw[..., n_heads:].reshape(b, nct, c, ng, sub, pk)
    gr = jnp.transpose(araw, (0, 3, 1, 4, 5, 2)).reshape(b, ng, nct, sub, pk * c)
    al = a_log.astype(F32).reshape(ng, hb)
    dt = dt_bias.astype(F32).reshape(ng, hb)
    zeros = jnp.zeros((ng, hb), F32)
    pc = jnp.stack([jnp.concatenate([zeros, al], axis=1),
                    jnp.concatenate([zeros, dt], axis=1)], axis=1)
    rep = lambda p: jnp.repeat(p.reshape(ng, sub, pk), c, axis=-1)
    pr = jnp.concatenate([rep(al), rep(dt)], axis=1)
    cw = conv_w.astype(F32)
    lev = jnp.asarray(_delta_level_map())
    rows = pk * c
    kern = functools.partial(_delta_kernel, hb=hb, tb=tb)
    return pl.pallas_call(
        kern,
        grid=(b, ng, seq // tb),
        in_specs=[
            pl.BlockSpec((None, tb, w), lambda bi, g, t: (bi, t, g)),
            pl.BlockSpec((None, tb, w), lambda bi, g, t: (bi, t, ng + g)),
            pl.BlockSpec((None, tb, w), lambda bi, g, t: (bi, t, 2 * ng + g)),
            pl.BlockSpec((None, tb, w), lambda bi, g, t: (bi, t, 3 * ng + g)),
            pl.BlockSpec((None, None, tb, 2 * hb), lambda bi, g, t: (bi, g, t, 0)),
            pl.BlockSpec((None, None, tb // c, sub, rows), lambda bi, g, t: (bi, g, t, 0, 0)),
            pl.BlockSpec((CONV_K, w), lambda bi, g, t: (0, g)),
            pl.BlockSpec((CONV_K, w), lambda bi, g, t: (0, ng + g)),
            pl.BlockSpec((CONV_K, w), lambda bi, g, t: (0, 2 * ng + g)),
            pl.BlockSpec((None, 2, 2 * hb), lambda bi, g, t: (g, 0, 0)),
            pl.BlockSpec((None, 2 * sub, rows), lambda bi, g, t: (g, 0, 0)),
            pl.BlockSpec((1, dk), lambda bi, g, t: (0, 0)),
            pl.BlockSpec((rows, rows), lambda bi, g, t: (0, 0)),
        ],
        out_specs=pl.BlockSpec((None, tb, w), lambda bi, g, t: (bi, t, g)),
        out_shape=jax.ShapeDtypeStruct((b, seq, n_heads * dk), BF16),
        scratch_shapes=[pltpu.VMEM((hb, dk, dk), F32), pltpu.VMEM((3, 8, w), F32)],
        compiler_params=_cparams(3),
        name=name,
    )(proj, proj, proj, proj, bgc, gr, cw, cw, cw, pc, pr, norm_w.astype(F32).reshape(1, dk), lev)


def _ffn(x32, x16, w_gate, w_up, w_down, gain, bias, layer):
    d, dff = w_gate.shape
    pad = (-dff) % 1024
    wg = jnp.pad(w_gate, ((0, 0), (0, pad))).astype(BF16)
    wu = jnp.pad(w_up, ((0, 0), (0, pad))).astype(BF16)
    wd = jnp.pad(w_down, ((0, pad), (0, 0))).astype(BF16)
    h = swiglu_up(x16, wg, wu, tm=1024, tn=512, name=f"ffn_up_{layer}")
    y = matmul_resid(h, wd, x32, ALPHA, tm=1024, tn=1024, tk=1024, name=f"ffn_down_{layer}")
    return layer_norm(y, gain, bias, tm=256, name=f"ffn_ln_{layer}")


def kernel(x, ab_w_in, ab_rel_bias, ab_lam_re, ab_lam_im, ab_log_step, ab_b_re, ab_b_im, ab_c_re, ab_c_im, ab_d_skip, ab_w_glu, ab_b_glu, ab_w_out, c_w_in, c_conv, c_a_log, c_dt_bias, c_norm_w, c_w_out, ffn_w_gate, ffn_w_up, ffn_w_down, ln_gain, ln_bias):
    b, seq, d = x.shape
    t = b * seq
    x32 = x.reshape(t, d).astype(F32)
    x16 = x32.astype(BF16)

    n_heads_a = ab_rel_bias.shape[1]
    d_attn = n_heads_a * DH_ATTN
    w_in = ab_w_in[0].astype(BF16)
    qkv = matmul(x16, w_in[:, :3 * d_attn], BF16, tm=1024, tn=512, name="ab_in_qkv")
    u = matmul(x16, w_in[:, 3 * d_attn:], F32, tm=1024, tn=512, name="ab_in_u")
    d_ssm = u.shape[1]
    y_a = attention(qkv.reshape(b, seq, 3 * d_attn), ab_rel_bias[0], n_heads_a, hb=2, name="ab_attn")
    y_s = ssm(u.reshape(b, seq, d_ssm), ab_lam_re[0], ab_lam_im[0], ab_log_step[0], ab_b_re[0],
              ab_b_im[0], ab_c_re[0], ab_c_im[0], gb=8, name="ab_ssm")
    y_b = ssm_glu(y_s.reshape(t, d_ssm), u, ab_d_skip[0].reshape(-1), ab_w_glu[0].astype(BF16),
                  ab_b_glu[0], tm=512, name="ab_glu")
    y_ab = jnp.concatenate([y_a.reshape(t, d_attn), y_b], axis=-1)
    y = matmul_resid(y_ab, ab_w_out[0].astype(BF16), x32, ALPHA, tm=1024, tn=1024, tk=1024,
                     name="ab_out")
    x32, x16 = layer_norm(y, ln_gain[0, 0], ln_bias[0, 0], tm=256, name="ab_ln")
    x32, x16 = _ffn(x32, x16, ffn_w_gate[0], ffn_w_up[0], ffn_w_down[0], ln_gain[0, 1], ln_bias[0, 1], 0)

    n_heads_c = c_a_log.shape[1]
    d_delta = n_heads_c * DK_DELTA
    cw_in = c_w_in[0]
    proj = matmul(x16, cw_in[:, :4 * d_delta].astype(BF16), BF16, tm=1024, tn=512, name="c_in")
    braw = matmul(x16, cw_in[:, 4 * d_delta:].astype(BF16), F32, tm=1024, tn=2 * n_heads_c,
                  name="c_in_small")
    o = gated_deltanet(proj.reshape(b, seq, 4 * d_delta), braw.reshape(b, seq, 2 * n_heads_c),
                       c_conv[0], c_a_log[0], c_dt_bias[0], c_norm_w[0], n_heads_c,
                       hb=16, tb=512, name="c_delta")
    y = matmul_resid(o.reshape(t, d_delta), c_w_out[0].astype(BF16), x32, ALPHA, tm=1024, tn=1024,
                     tk=1024, name="c_out")
    x32, x16 = layer_norm(y, ln_gain[1, 0], ln_bias[1, 0], tm=256, name="c_ln")
    x32, x16 = _ffn(x32, x16, ffn_w_gate[1], ffn_w_up[1], ffn_w_down[1], ln_gain[1, 1], ln_bias[1, 1], 1)
    return x32.reshape(b, seq, d).astype(x.dtype)
```

```python
import functools
import math

import jax
import jax.numpy as jnp
from jax import lax
from jax.experimental import pallas as pl
from jax.experimental.pallas import tpu as pltpu

F32 = jnp.float32
BF16 = jnp.bfloat16

CHUNK = 64
LEFT_CHUNKS = 8
DH_ATTN = 128
MAX_REL = 256
SSM_GROUP = 16
SSM_STATE = 64
DK_DELTA = 128
CONV_K = 4
DEPTH = 2
ALPHA = (2.0 * DEPTH) ** 0.25
LN_EPS = 1e-5
NORM_EPS = 1e-6
NEG_BIG = -1e30

V7X_VMEM_LIMIT_BYTES = 56 * 1024 * 1024
LANES = 128
SSM_BLOCK = 16
HI = lax.Precision.HIGHEST


def _cparams(n_axes, flags=None):
    return pltpu.CompilerParams(
        dimension_semantics=("arbitrary",) * n_axes,
        vmem_limit_bytes=V7X_VMEM_LIMIT_BYTES,
        flags=flags,
    )


def _sigmoid(x):
    return 1.0 / (1.0 + jnp.exp(-x))


def _mm_kernel(x_ref, w_ref, o_ref):
    w = w_ref[...].astype(BF16)
    o_ref[...] = jnp.dot(x_ref[...], w, preferred_element_type=F32).astype(o_ref.dtype)


def matmul(x, w, out_dtype, *, col0, n, tm, tn, name):
    m, k = x.shape
    assert m % tm == 0 and n % tn == 0 and col0 % tn == 0
    j0 = col0 // tn
    return pl.pallas_call(
        _mm_kernel,
        grid=(m // tm, n // tn),
        in_specs=[
            pl.BlockSpec((tm, k), lambda i, j: (i, 0)),
            pl.BlockSpec((k, tn), lambda i, j: (0, j0 + j)),
        ],
        out_specs=pl.BlockSpec((tm, tn), lambda i, j: (i, j)),
        out_shape=jax.ShapeDtypeStruct((m, n), out_dtype),
        compiler_params=_cparams(2),
        name=name,
    )(x, w)


def _mm_resid_kernel(x_ref, w_ref, r_ref, o_ref, acc_ref, *, alpha, nk, tk, k_valid):
    kk = pl.program_id(2)

    @pl.when(kk == 0)
    def _():
        acc_ref[...] = alpha * r_ref[...]

    w = w_ref[...]
    if k_valid % tk:
        row = lax.broadcasted_iota(jnp.int32, w.shape, 0) + kk * tk
        w = jnp.where(row < k_valid, w, 0.0)
    acc_ref[...] += jnp.dot(x_ref[...], w.astype(BF16), preferred_element_type=F32)

    @pl.when(kk == nk - 1)
    def _():
        o_ref[...] = acc_ref[...]


def matmul_resid(x, w, resid, alpha, *, tm, tn, tk, name):
    m, k = x.shape
    kw, n = w.shape
    assert m % tm == 0 and n % tn == 0 and k % tk == 0 and k - kw < tk
    nk = k // tk
    return pl.pallas_call(
        functools.partial(_mm_resid_kernel, alpha=alpha, nk=nk, tk=tk, k_valid=kw),
        grid=(m // tm, n // tn, nk),
        in_specs=[
            pl.BlockSpec((tm, tk), lambda i, j, kk: (i, kk)),
            pl.BlockSpec((tk, tn), lambda i, j, kk: (kk, j)),
            pl.BlockSpec((tm, tn), lambda i, j, kk: (i, j)),
        ],
        out_specs=pl.BlockSpec((tm, tn), lambda i, j, kk: (i, j)),
        out_shape=jax.ShapeDtypeStruct((m, n), F32),
        scratch_shapes=[pltpu.VMEM((tm, tn), F32)],
        compiler_params=_cparams(3),
        name=name,
    )(x, w, resid)


def _swiglu_kernel(x_ref, wg_ref, wu_ref, o_ref, *, n_valid_blocks):
    x = x_ref[...]
    g = jnp.dot(x, wg_ref[...].astype(BF16), preferred_element_type=F32)
    u = jnp.dot(x, wu_ref[...].astype(BF16), preferred_element_type=F32)
    h = g * _sigmoid(g) * u
    h = jnp.where(pl.program_id(1) < n_valid_blocks, h, 0.0)
    o_ref[...] = h.astype(o_ref.dtype)


def swiglu_up(x, wg, wu, *, n_out, tm, tn, name):
    m, k = x.shape
    n = wg.shape[1]
    assert m % tm == 0 and n % tn == 0 and n_out % tn == 0 and n_out >= n
    nvb = n // tn
    wmap = lambda i, j: (0, jnp.minimum(j, nvb - 1))
    return pl.pallas_call(
        functools.partial(_swiglu_kernel, n_valid_blocks=nvb),
        grid=(m // tm, n_out // tn),
        in_specs=[
            pl.BlockSpec((tm, k), lambda i, j: (i, 0)),
            pl.BlockSpec((k, tn), wmap),
            pl.BlockSpec((k, tn), wmap),
        ],
        out_specs=pl.BlockSpec((tm, tn), lambda i, j: (i, j)),
        out_shape=jax.ShapeDtypeStruct((m, n_out), BF16),
        compiler_params=_cparams(2),
        name=name,
    )(x, wg, wu)


def _ln_kernel(y_ref, g_ref, b_ref, o32_ref, o16_ref):
    y = y_ref[...]
    mu = jnp.mean(y, axis=-1, keepdims=True)
    yc = y - mu
    var = jnp.mean(yc * yc, axis=-1, keepdims=True)
    x = yc * lax.rsqrt(var + LN_EPS) * g_ref[...] + b_ref[...]
    o32_ref[...] = x
    o16_ref[...] = x.astype(BF16)


def layer_norm(y, gain, bias, *, tm, name):
    m, d = y.shape
    assert m % tm == 0
    return pl.pallas_call(
        _ln_kernel,
        grid=(m // tm,),
        in_specs=[
            pl.BlockSpec((tm, d), lambda i: (i, 0)),
            pl.BlockSpec((1, d), lambda i: (0, 0)),
            pl.BlockSpec((1, d), lambda i: (0, 0)),
        ],
        out_specs=[
            pl.BlockSpec((tm, d), lambda i: (i, 0)),
            pl.BlockSpec((tm, d), lambda i: (i, 0)),
        ],
        out_shape=[jax.ShapeDtypeStruct((m, d), F32), jax.ShapeDtypeStruct((m, d), BF16)],
        compiler_params=_cparams(1),
        name=name,
    )(y, gain.reshape(1, d), bias.reshape(1, d))


ATTN_QB = 4 * CHUNK
ATTN_WIN = ATTN_QB + LEFT_CHUNKS * CHUNK
ATTN_NVAR = LEFT_CHUNKS * CHUNK // ATTN_QB + 1


def _attn_bias_tiles(rel_bias):
    n_heads = rel_bias.shape[0]
    period = ATTN_QB + ATTN_WIN
    m = jnp.arange(period)
    tiles = []
    for v in range(ATTN_NVAR):
        rel = v * ATTN_QB - jnp.where(m < ATTN_WIN, m, m - period)
        row = rel_bias.astype(F32)[:, jnp.clip(rel, -MAX_REL, MAX_REL) + MAX_REL]
        flat = jnp.tile(row, (1, ATTN_QB))[:, :ATTN_QB * (period - 1)]
        skew = flat.reshape(n_heads, ATTN_QB, period - 1)[:, :, :ATTN_WIN]
        tiles.append(skew)
    tiles = jnp.stack(tiles)
    i = jnp.arange(ATTN_QB)[None, :, None]
    j = jnp.arange(ATTN_WIN)[None, None, :]
    qpos = (jnp.arange(ATTN_NVAR) * ATTN_QB)[:, None, None] + i
    dchunk = qpos // CHUNK - j // CHUNK
    valid = (dchunk >= 0) & (dchunk <= LEFT_CHUNKS)
    return jnp.where(valid[:, None], tiles, NEG_BIG)


def _attn_kernel(q_ref, k_ref, v_ref, bias_ref, o_ref, *, hb, seq):
    scale = DH_ATTN ** -0.5
    nq = seq // ATTN_QB

    def body(qi, carry):
        qs = pl.multiple_of(qi * ATTN_QB, ATTN_QB)
        start = pl.multiple_of(jnp.maximum(qs - LEFT_CHUNKS * CHUNK, 0), ATTN_QB)
        var = jnp.minimum(qi, ATTN_NVAR - 1)
        heads = [slice(h * DH_ATTN, (h + 1) * DH_ATTN) for h in range(hb)]
        s_all = [lax.dot_general(q_ref[pl.ds(qs, ATTN_QB), c], k_ref[pl.ds(start, ATTN_WIN), c],
                                 (((1,), (1,)), ((), ())), preferred_element_type=F32)
                 for c in heads]
        p_all = []
        for h, s in enumerate(s_all):
            s = s * scale + bias_ref[var, h]
            m = jnp.max(s, axis=-1, keepdims=True)
            p = jnp.exp(s - m)
            l = jnp.sum(p, axis=-1, keepdims=True)
            p_all.append((p / l).astype(BF16))
        for c, p in zip(heads, p_all):
            o = jnp.dot(p, v_ref[pl.ds(start, ATTN_WIN), c], preferred_element_type=F32)
            o_ref[pl.ds(qs, ATTN_QB), c] = o.astype(o_ref.dtype)
        return carry

    lax.fori_loop(0, nq, body, 0)


def attention(qkv, rel_bias, n_heads, *, hb, out_cols, name):
    b, seq, _ = qkv.shape
    assert n_heads % hb == 0 and seq % ATTN_QB == 0 and seq >= ATTN_WIN
    ng = n_heads // hb
    wcols = hb * DH_ATTN
    bias = _attn_bias_tiles(rel_bias)
    kern = functools.partial(_attn_kernel, hb=hb, seq=seq)
    return pl.pallas_call(
        kern,
        grid=(b, ng),
        in_specs=[
            pl.BlockSpec((None, seq, wcols), lambda bi, g: (bi, 0, g)),
            pl.BlockSpec((None, seq, wcols), lambda bi, g: (bi, 0, ng + g)),
            pl.BlockSpec((None, seq, wcols), lambda bi, g: (bi, 0, 2 * ng + g)),
            pl.BlockSpec((ATTN_NVAR, hb, ATTN_QB, ATTN_WIN), lambda bi, g: (0, g, 0, 0)),
        ],
        out_specs=pl.BlockSpec((None, seq, wcols), lambda bi, g: (bi, 0, g)),
        out_shape=jax.ShapeDtypeStruct((b, seq, out_cols), BF16),
        compiler_params=_cparams(2),
        name=name,
    )(qkv, qkv, qkv, bias)


def _ssm_tables(lam_re, lam_im, log_step, b_re, b_im, c_re, c_im, n_scan_steps):
    g, p = lam_re.shape
    hg = b_re.shape[-1]
    tc = SSM_BLOCK
    lam = lax.complex(jnp.minimum(lam_re.astype(F32), -1e-4), lam_im.astype(F32))
    step = jnp.exp(log_step.astype(F32))[:, None]
    lam_bar = jnp.exp(lam * step)
    b_bar = ((lam_bar - 1.0) / lam)[..., None] * lax.complex(b_re.astype(F32), b_im.astype(F32))
    pw = [jnp.ones_like(lam_bar)]
    for _ in range(tc):
        pw.append(pw[-1] * lam_bar)
    pw = jnp.stack(pw)
    xb = pw[:tc, :, :, None] * b_bar[None]
    xb_re, xb_im = jnp.real(xb), jnp.imag(xb)
    cr, ci = c_re.astype(F32), c_im.astype(F32)
    kmat = (jnp.einsum('ghp,tgpi->tghi', cr, xb_re, precision=HI)
            - jnp.einsum('ghp,tgpi->tghi', ci, xb_im, precision=HI))
    s_idx = jnp.arange(tc)[:, None]
    t_idx = jnp.arange(tc)[None, :]
    lag = t_idx - s_idx
    kst = kmat[jnp.clip(lag, 0, tc - 1)]
    kst = jnp.where((lag >= 0)[:, :, None, None, None], kst, 0.0)
    m_tab = jnp.transpose(kst, (2, 0, 4, 1, 3)).reshape(g, tc * hg, tc * hg)
    xr = jnp.transpose(xb_re[::-1], (1, 0, 3, 2)).reshape(g, tc * hg, p)
    xi = jnp.transpose(xb_im[::-1], (1, 0, 3, 2)).reshape(g, tc * hg, p)
    f_tab = jnp.concatenate([xr, xi], axis=-1)
    cc = lax.complex(cr, ci)
    gmat = cc[None] * pw[1:, :, None, :]
    e_re = jnp.transpose(jnp.real(gmat), (1, 3, 0, 2)).reshape(g, p, tc * hg)
    e_im = jnp.transpose(-jnp.imag(gmat), (1, 3, 0, 2)).reshape(g, p, tc * hg)
    e_tab = jnp.concatenate([e_re, e_im], axis=1)
    a = pw[tc]
    a1, a2 = [], []
    for _ in range(n_scan_steps):
        a1.append(jnp.concatenate([jnp.real(a), jnp.real(a)], axis=-1))
        a2.append(jnp.concatenate([-jnp.imag(a), jnp.imag(a)], axis=-1))
        a = a * a
    pad = (-n_scan_steps) % 8
    a1 = jnp.pad(jnp.stack(a1, axis=1), ((0, 0), (0, pad), (0, 0)))
    a2 = jnp.pad(jnp.stack(a2, axis=1), ((0, 0), (0, pad), (0, 0)))
    return m_tab, f_tab, e_tab, a1, a2


def _ssm_kernel(u_ref, m_ref, f_ref, e_ref, a1_ref, a2_ref, y_ref, *, gb, nb, nc, nsteps):
    p2 = 2 * SSM_STATE
    rows = lax.broadcasted_iota(jnp.int32, (nc, p2), 0)

    def per_group(g, carry):
        u = u_ref[g]
        sloc = jnp.dot(u, f_ref[g], preferred_element_type=F32)
        prevs = []
        for b in range(nb):
            x = sloc[b * nc:(b + 1) * nc]
            for k in range(nsteps):
                d = 1 << k
                sh = jnp.where(rows >= d, pltpu.roll(x, d, 0), 0.0)
                sw = pltpu.roll(sh, SSM_STATE, 1)
                x = x + a1_ref[g, k:k + 1, :] * sh + a2_ref[g, k:k + 1, :] * sw
            prevs.append(jnp.where(rows >= 1, pltpu.roll(x, 1, 0), 0.0))
        sprev = jnp.concatenate(prevs, axis=0) if nb > 1 else prevs[0]
        y = jnp.dot(u, m_ref[g], preferred_element_type=F32)
        y = y + jnp.dot(sprev.astype(BF16), e_ref[g], preferred_element_type=F32)
        y_ref[g] = y
        return carry

    lax.fori_loop(0, gb, per_group, 0)


def ssm(u, lam_re, lam_im, log_step, b_re, b_im, c_re, c_im, *, gb, name):
    b, seq, ds = u.shape
    g = lam_re.shape[0]
    hg = ds // g
    tc = SSM_BLOCK
    nc = seq // tc
    assert seq % tc == 0 and g % gb == 0 and nc & (nc - 1) == 0
    nsteps = nc.bit_length() - 1
    m_tab, f_tab, e_tab, a1, a2 = _ssm_tables(lam_re, lam_im, log_step, b_re, b_im, c_re, c_im, nsteps)
    ut = u.reshape(b, nc, tc, g, hg).transpose(3, 0, 1, 2, 4).reshape(g, b * nc, tc * hg).astype(BF16)
    kw = tc * hg
    p2 = 2 * SSM_STATE
    sp = a1.shape[1]
    kern = functools.partial(_ssm_kernel, gb=gb, nb=b, nc=nc, nsteps=nsteps)
    yt = pl.pallas_call(
        kern,
        grid=(g // gb,),
        in_specs=[
            pl.BlockSpec((gb, b * nc, kw), lambda i: (i, 0, 0)),
            pl.BlockSpec((gb, kw, kw), lambda i: (i, 0, 0)),
            pl.BlockSpec((gb, kw, p2), lambda i: (i, 0, 0)),
            pl.BlockSpec((gb, p2, kw), lambda i: (i, 0, 0)),
            pl.BlockSpec((gb, sp, p2), lambda i: (i, 0, 0)),
            pl.BlockSpec((gb, sp, p2), lambda i: (i, 0, 0)),
        ],
        out_specs=pl.BlockSpec((gb, b * nc, kw), lambda i: (i, 0, 0)),
        out_shape=jax.ShapeDtypeStruct((g, b * nc, kw), F32),
        compiler_params=_cparams(1),
        name=name,
    )(ut, m_tab.astype(BF16), f_tab.astype(BF16), e_tab.astype(BF16), a1, a2)
    return yt.reshape(g, b, nc, tc, hg).transpose(1, 2, 3, 0, 4).reshape(b, seq, ds)


def _gelu_tanh(x):
    c = math.sqrt(2.0 / math.pi)
    return 0.5 * x * (1.0 + jnp.tanh(c * (x + 0.044715 * (x * x * x))))


def _glu_kernel(y_ref, u_ref, d_ref, w_ref, b_ref, dst_ref, o_ref):
    del dst_ref
    yb = _gelu_tanh(y_ref[...] + d_ref[...] * u_ref[...])
    z = jnp.dot(yb.astype(BF16), w_ref[...], preferred_element_type=F32) + b_ref[...]
    o_ref[...] = (yb * _sigmoid(z)).astype(o_ref.dtype)


def ssm_glu(y, u, d_skip, w_glu, b_glu, dst, *, tm, name):
    m, d = y.shape
    assert m % tm == 0 and dst.shape[0] == m and dst.shape[1] % d == 0 and dst.dtype == BF16
    col_blk = dst.shape[1] // d - 1
    return pl.pallas_call(
        _glu_kernel,
        grid=(m // tm,),
        in_specs=[
            pl.BlockSpec((tm, d), lambda i: (i, 0)),
            pl.BlockSpec((tm, d), lambda i: (i, 0)),
            pl.BlockSpec((1, d), lambda i: (0, 0)),
            pl.BlockSpec((d, d), lambda i: (0, 0)),
            pl.BlockSpec((1, d), lambda i: (0, 0)),
            pl.BlockSpec(memory_space=pl.ANY),
        ],
        out_specs=pl.BlockSpec((tm, d), lambda i: (i, col_blk)),
        out_shape=jax.ShapeDtypeStruct(dst.shape, BF16),
        input_output_aliases={5: 0},
        compiler_params=_cparams(1),
        name=name,
    )(y, u, d_skip.reshape(1, d).astype(F32), w_glu, b_glu.reshape(1, d).astype(F32), dst)


def _softplus(x):
    return jnp.maximum(x, 0.0) + jnp.log(1.0 + jnp.exp(-jnp.abs(x)))


def _cumsum_rows(x, n):
    rows = lax.broadcasted_iota(jnp.int32, x.shape, 0)
    d = 1
    while d < n:
        x = x + jnp.where(rows >= d, pltpu.roll(x, d, 0), 0.0)
        d *= 2
    return x


def _cumsum_lane_segments(x, seg):
    pos = lax.broadcasted_iota(jnp.int32, x.shape, 1) & (seg - 1)
    d = 1
    while d < seg:
        x = x + jnp.where(pos >= d, pltpu.roll(x, d, 1), 0.0)
        d *= 2
    return x


DELTA_PACK = 4
DELTA_ROWS = DELTA_PACK * CHUNK
DELTA_LEVELS = CHUNK.bit_length() - 1


def _delta_level_map():
    import numpy as np
    i = np.arange(DELTA_ROWS)[:, None]
    j = np.arange(DELTA_ROWS)[None, :]
    x = i ^ j
    lev = np.zeros_like(x)
    for a in range(1, 8):
        lev += (x >> a) != 0
    out = np.where((i > j) & ((i // CHUNK) == (j // CHUNK)), lev, 99)
    out = np.where(i == j, -1, out)
    return out.astype(np.int32)


def _delta_kernel(q_ref, k_ref, v_ref, gate_ref, bgc_ref, gr_ref, cwq_ref, cwk_ref, cwv_ref,
                  pc_ref, pr_ref, nw_ref, lev_ref, o_ref, s_ref, halo_ref, *, hb, tb):
    c = CHUNK
    dk = DK_DELTA
    pk = DELTA_PACK
    rows = DELTA_ROWS
    sub = hb // pk
    scale = dk ** -0.5
    n_chunks = tb // c
    ti = pl.program_id(2)

    @pl.when(ti == 0)
    def _():
        s_ref[...] = jnp.zeros_like(s_ref)
        halo_ref[...] = jnp.zeros_like(halo_ref)

    alog_c = pc_ref[0:1, :]
    dtb_c = pc_ref[1:2, :]
    alog_r = pr_ref[0:sub, :]
    dtb_r = pr_ref[sub:2 * sub, :]
    nw = nw_ref[...]

    def conv_silu(win, cw_ref):
        acc = (cw_ref[3:4, :] * win[8:8 + c] + cw_ref[2:3, :] * win[7:7 + c]
               + cw_ref[1:2, :] * win[6:6 + c] + cw_ref[0:1, :] * win[5:5 + c])
        return acc * _sigmoid(acc)

    def stack_heads(x, s):
        return jnp.concatenate(
            [x[:, (pk * s + j) * dk:(pk * s + j + 1) * dk] for j in range(pk)], axis=0)

    def stack_cols(x, s, off):
        return jnp.concatenate(
            [x[:, off + pk * s + j:off + pk * s + j + 1] for j in range(pk)], axis=0)

    def chunk_body(ch, tails):
        tq, tk, tv = tails
        r0 = pl.multiple_of(ch * c, c)
        curq = q_ref[pl.ds(r0, c), :].astype(F32)
        curk = k_ref[pl.ds(r0, c), :].astype(F32)
        curv = v_ref[pl.ds(r0, c), :].astype(F32)
        qa = conv_silu(jnp.concatenate([tq, curq], axis=0), cwq_ref)
        ka = conv_silu(jnp.concatenate([tk, curk], axis=0), cwk_ref)
        va = conv_silu(jnp.concatenate([tv, curv], axis=0), cwv_ref)
        gate = gate_ref[pl.ds(r0, c), :].astype(F32)

        bg = bgc_ref[pl.ds(r0, c), :]
        beta_all = _sigmoid(bg)
        g_all = -jnp.exp(alog_c) * _softplus(bg + dtb_c)
        gc_all = _cumsum_rows(g_all, c)
        gl_all = jnp.broadcast_to(gc_all[c - 1:c, :], gc_all.shape)
        g_row = -jnp.exp(alog_r) * _softplus(gr_ref[ch] + dtb_r)
        gcr_all = _cumsum_lane_segments(g_row, c)

        lev = lev_ref[...]
        subs = []
        for s in range(sub):
            q = stack_heads(qa, s)
            k = stack_heads(ka, s)
            v = stack_heads(va, s)
            q = q * lax.rsqrt(jnp.sum(q * q, axis=-1, keepdims=True) + NORM_EPS)
            k = k * lax.rsqrt(jnp.sum(k * k, axis=-1, keepdims=True) + NORM_EPS)
            qs = q * scale
            beta = stack_cols(beta_all, s, 0)
            gc_col = stack_cols(gc_all, s, hb)
            gl_col = stack_cols(gl_all, s, hb)
            gc_row = gcr_all[s:s + 1, :]
            egc = jnp.exp(gc_col)
            kb = k * beta
            lhs = jnp.concatenate([kb, qs], axis=0).astype(BF16)
            kq = lax.dot_general(lhs, k.astype(BF16), (((1,), (1,)), ((), ())),
                                 preferred_element_type=F32)
            dec = jnp.exp(gc_col - gc_row)
            nm = kq[:rows] * dec
            subs.append(dict(
                nm=nm, qg=qs * egc,
                attn=jnp.where(lev < DELTA_LEVELS, kq[rows:] * dec, 0.0),
                x=jnp.concatenate([v * beta, kb * egc], axis=1),
                kd=(k * jnp.exp(gl_col - gc_col)).astype(BF16),
                t=jnp.where(lev == 0, -nm, jnp.where(lev == -1, 1.0, 0.0))))
        for a in range(1, DELTA_LEVELS):
            for st in subs:
                cm = jnp.where(lev == a, st["nm"], 0.0).astype(BF16)
                tb16 = st["t"].astype(BF16)
                pm = jnp.dot(tb16, cm, preferred_element_type=F32)
                st["t"] = st["t"] - jnp.dot(pm.astype(BF16), tb16, preferred_element_type=F32)
        for st in subs:
            t_off = jnp.where(lev == -1, 0.0, st["t"]).astype(BF16)
            x = st["x"]
            st["x"] = x + jnp.dot(t_off, x.astype(BF16), preferred_element_type=F32)
        for s, st in enumerate(subs):
            u_c = st["x"][:, :dk]
            w_c = st["x"][:, dk:]
            qg = st["qg"]
            ws_parts, qs_parts = [], []
            for j in range(pk):
                h = pk * s + j
                rs = slice(j * c, (j + 1) * c)
                lhs2 = jnp.concatenate([w_c[rs], qg[rs]], axis=0).astype(BF16)
                ws = jnp.dot(lhs2, s_ref[h].astype(BF16), preferred_element_type=F32)
                ws_parts.append(ws[:c])
                qs_parts.append(ws[c:])
            v_new = u_c - jnp.concatenate(ws_parts, axis=0)
            out = jnp.concatenate(qs_parts, axis=0) + jnp.dot(
                st["attn"].astype(BF16), v_new.astype(BF16), preferred_element_type=F32)
            ri = lax.broadcasted_iota(jnp.int32, (rows, pk * dk), 0)
            cj = lax.broadcasted_iota(jnp.int32, (rows, pk * dk), 1)
            vbd = jnp.where((ri // c) == (cj // dk), jnp.concatenate([v_new] * pk, axis=1), 0.0)
            upd = lax.dot_general(st["kd"], vbd.astype(BF16), (((0,), (0,)), ((), ())),
                                  preferred_element_type=F32)
            o = out * lax.rsqrt(jnp.mean(out * out, axis=-1, keepdims=True) + NORM_EPS) * nw
            gt = stack_heads(gate, s)
            res = (o * (gt * _sigmoid(gt))).astype(o_ref.dtype)
            for j in range(pk):
                h = pk * s + j
                e_last = jnp.exp(gc_all[c - 1:c, hb + h:hb + h + 1])
                s_ref[h] = s_ref[h] * e_last + upd[:, j * dk:(j + 1) * dk]
                o_ref[pl.ds(r0, c), h * dk:(h + 1) * dk] = res[j * c:(j + 1) * c]

        return (curq[c - 8:], curk[c - 8:], curv[c - 8:])

    tails0 = (halo_ref[0], halo_ref[1], halo_ref[2])
    tq, tk, tv = lax.fori_loop(0, n_chunks, chunk_body, tails0)
    halo_ref[0] = tq
    halo_ref[1] = tk
    halo_ref[2] = tv


def gated_deltanet(proj, braw, conv_w, a_log, dt_bias, norm_w, n_heads, *, hb, tb, name):
    b, seq, _ = proj.shape
    dk = DK_DELTA
    c = CHUNK
    pk = DELTA_PACK
    assert n_heads % hb == 0 and hb % pk == 0 and seq % tb == 0 and tb % c == 0
    ng = n_heads // hb
    sub = hb // pk
    w = hb * dk
    nct = seq // c
    br = braw.reshape(b, seq, 2, ng, hb)
    bgc = jnp.transpose(br, (0, 3, 1, 2, 4)).reshape(b, ng, seq, 2 * hb)
    araw = braw[..., n_heads:].reshape(b, nct, c, ng, sub, pk)
    gr = jnp.transpose(araw, (0, 3, 1, 4, 5, 2)).reshape(b, ng, nct, sub, pk * c)
    al = a_log.astype(F32).reshape(ng, hb)
    dt = dt_bias.astype(F32).reshape(ng, hb)
    zeros = jnp.zeros((ng, hb), F32)
    pc = jnp.stack([jnp.concatenate([zeros, al], axis=1),
                    jnp.concatenate([zeros, dt], axis=1)], axis=1)
    rep = lambda p: jnp.repeat(p.reshape(ng, sub, pk), c, axis=-1)
    pr = jnp.concatenate([rep(al), rep(dt)], axis=1)
    cw = conv_w.astype(F32)
    lev = jnp.asarray(_delta_level_map())
    rows = pk * c
    kern = functools.partial(_delta_kernel, hb=hb, tb=tb)
    return pl.pallas_call(
        kern,
        grid=(b, ng, seq // tb),
        in_specs=[
            pl.BlockSpec((None, tb, w), lambda bi, g, t: (bi, t, g)),
            pl.BlockSpec((None, tb, w), lambda bi, g, t: (bi, t, ng + g)),
            pl.BlockSpec((None, tb, w), lambda bi, g, t: (bi, t, 2 * ng + g)),
            pl.BlockSpec((None, tb, w), lambda bi, g, t: (bi, t, 3 * ng + g)),
            pl.BlockSpec((None, None, tb, 2 * hb), lambda bi, g, t: (bi, g, t, 0)),
            pl.BlockSpec((None, None, tb // c, sub, rows), lambda bi, g, t: (bi, g, t, 0, 0)),
            pl.BlockSpec((CONV_K, w), lambda bi, g, t: (0, g)),
            pl.BlockSpec((CONV_K, w), lambda bi, g, t: (0, ng + g)),
            pl.BlockSpec((CONV_K, w), lambda bi, g, t: (0, 2 * ng + g)),
            pl.BlockSpec((None, 2, 2 * hb), lambda bi, g, t: (g, 0, 0)),
            pl.BlockSpec((None, 2 * sub, rows), lambda bi, g, t: (g, 0, 0)),
            pl.BlockSpec((1, dk), lambda bi, g, t: (0, 0)),
            pl.BlockSpec((rows, rows), lambda bi, g, t: (0, 0)),
        ],
        out_specs=pl.BlockSpec((None, tb, w), lambda bi, g, t: (bi, t, g)),
        out_shape=jax.ShapeDtypeStruct((b, seq, n_heads * dk), BF16),
        scratch_shapes=[pltpu.VMEM((hb, dk, dk), F32), pltpu.VMEM((3, 8, w), F32)],
        compiler_params=_cparams(3),
        name=name,
    )(proj, proj, proj, proj, bgc, gr, cw, cw, cw, pc, pr, norm_w.astype(F32).reshape(1, dk), lev)


FFN_TILE = 1024


def _ffn(x32, x16, w_gate, w_up, w_down, gain, bias, layer):
    dff = w_gate.shape[1]
    dff_pad = -(-dff // FFN_TILE) * FFN_TILE
    h = swiglu_up(x16, w_gate, w_up, n_out=dff_pad, tm=1024, tn=256, name=f"ffn_up_{layer}")
    y = matmul_resid(h, w_down, x32, ALPHA, tm=1024, tn=1024, tk=FFN_TILE, name=f"ffn_down_{layer}")
    return layer_norm(y, gain, bias, tm=256, name=f"ffn_ln_{layer}")


def kernel(x, ab_w_in, ab_rel_bias, ab_lam_re, ab_lam_im, ab_log_step, ab_b_re, ab_b_im, ab_c_re, ab_c_im, ab_d_skip, ab_w_glu, ab_b_glu, ab_w_out, c_w_in, c_conv, c_a_log, c_dt_bias, c_norm_w, c_w_out, ffn_w_gate, ffn_w_up, ffn_w_down, ln_gain, ln_bias):
    b, seq, d = x.shape
    t = b * seq
    x32 = x.reshape(t, d).astype(F32)
    x16 = x32.astype(BF16)

    n_heads_a = ab_rel_bias.shape[1]
    d_attn = n_heads_a * DH_ATTN
    w_in = ab_w_in[0]
    d_ssm = w_in.shape[1] - 3 * d_attn
    qkv = matmul(x16, w_in, BF16, col0=0, n=3 * d_attn, tm=1024, tn=512, name="ab_in_qkv")
    u = matmul(x16, w_in, F32, col0=3 * d_attn, n=d_ssm, tm=1024, tn=512, name="ab_in_u")
    y_ab = attention(qkv.reshape(b, seq, 3 * d_attn), ab_rel_bias[0], n_heads_a, hb=2,
                     out_cols=d_attn + d_ssm, name="ab_attn")
    y_s = ssm(u.reshape(b, seq, d_ssm), ab_lam_re[0], ab_lam_im[0], ab_log_step[0], ab_b_re[0],
              ab_b_im[0], ab_c_re[0], ab_c_im[0], gb=8, name="ab_ssm")
    y_ab = ssm_glu(y_s.reshape(t, d_ssm), u, ab_d_skip[0].reshape(-1), ab_w_glu[0].astype(BF16),
                   ab_b_glu[0], y_ab.reshape(t, d_attn + d_ssm), tm=512, name="ab_glu")
    y = matmul_resid(y_ab, ab_w_out[0], x32, ALPHA, tm=1024, tn=1024, tk=1024, name="ab_out")
    x32, x16 = layer_norm(y, ln_gain[0, 0], ln_bias[0, 0], tm=256, name="ab_ln")
    x32, x16 = _ffn(x32, x16, ffn_w_gate[0], ffn_w_up[0], ffn_w_down[0], ln_gain[0, 1], ln_bias[0, 1], 0)

    n_heads_c = c_a_log.shape[1]
    d_delta = n_heads_c * DK_DELTA
    cw_in = c_w_in[0]
    proj = matmul(x16, cw_in, BF16, col0=0, n=4 * d_delta, tm=1024, tn=512, name="c_in")
    braw = matmul(x16, cw_in[:, 4 * d_delta:], F32, col0=0, n=2 * n_heads_c, tm=1024,
                  tn=2 * n_heads_c, name="c_in_small")
    o = gated_deltanet(proj.reshape(b, seq, 4 * d_delta), braw.reshape(b, seq, 2 * n_heads_c),
                       c_conv[0], c_a_log[0], c_dt_bias[0], c_norm_w[0], n_heads_c,
                       hb=16, tb=512, name="c_delta")
    y = matmul_resid(o.reshape(t, d_delta), c_w_out[0], x32, ALPHA, tm=1024, tn=1024, tk=1024,
                     name="c_out")
    x32, x16 = layer_norm(y, ln_gain[1, 0], ln_bias[1, 0], tm=256, name="c_ln")
    x32, x16 = _ffn(x32, x16, ffn_w_gate[1], ffn_w_up[1], ffn_w_down[1], ln_gain[1, 1], ln_bias[1, 1], 1)
    return x32.reshape(b, seq, d).astype(x.dtype)
```

```python
import functools
import math

import jax
import jax.numpy as jnp
from jax import lax
from jax.experimental import pallas as pl
from jax.experimental.pallas import tpu as pltpu

F32 = jnp.float32
BF16 = jnp.bfloat16

CHUNK = 64
LEFT_CHUNKS = 8
DH_ATTN = 128
MAX_REL = 256
SSM_GROUP = 16
SSM_STATE = 64
DK_DELTA = 128
CONV_K = 4
DEPTH = 2
ALPHA = (2.0 * DEPTH) ** 0.25
LN_EPS = 1e-5
NORM_EPS = 1e-6
NEG_BIG = -1e30

V7X_VMEM_LIMIT_BYTES = 56 * 1024 * 1024
LANES = 128
SSM_BLOCK = 16
HI = lax.Precision.HIGHEST


def _cparams(n_axes, flags=None):
    return pltpu.CompilerParams(
        dimension_semantics=("arbitrary",) * n_axes,
        vmem_limit_bytes=V7X_VMEM_LIMIT_BYTES,
        flags=flags,
    )


def _sigmoid(x):
    return 1.0 / (1.0 + jnp.exp(-x))


def _mm_kernel(x_ref, w_ref, o_ref):
    w = w_ref[...].astype(BF16)
    o_ref[...] = jnp.dot(x_ref[...], w, preferred_element_type=F32).astype(o_ref.dtype)


def matmul(x, w, layer, out_dtype, *, col0, n, tm, tn, name):
    m, k = x.shape
    assert m % tm == 0 and n % tn == 0 and col0 % tn == 0
    j0 = col0 // tn
    return pl.pallas_call(
        _mm_kernel,
        grid=(m // tm, n // tn),
        in_specs=[
            pl.BlockSpec((tm, k), lambda i, j: (i, 0), pipeline_mode=pl.Buffered(1)),
            pl.BlockSpec((None, k, tn), lambda i, j: (layer, 0, j0 + j)),
        ],
        out_specs=pl.BlockSpec((tm, tn), lambda i, j: (i, j)),
        out_shape=jax.ShapeDtypeStruct((m, n), out_dtype),
        compiler_params=_cparams(2),
        name=name,
    )(x, w)


def _mm_resid_kernel(x_ref, w_ref, r_ref, o_ref, acc_ref, *, alpha, nk, tk, k_valid):
    kk = pl.program_id(2)

    @pl.when(kk == 0)
    def _():
        acc_ref[...] = alpha * r_ref[...]

    w = w_ref[...]
    if k_valid % tk:
        row = lax.broadcasted_iota(jnp.int32, w.shape, 0) + kk * tk
        w = jnp.where(row < k_valid, w, 0.0)
    acc_ref[...] += jnp.dot(x_ref[...], w.astype(BF16), preferred_element_type=F32)

    @pl.when(kk == nk - 1)
    def _():
        o_ref[...] = acc_ref[...]


def matmul_resid(x, w, layer, resid, alpha, *, tm, tn, tk, name):
    m, k = x.shape
    _, kw, n = w.shape
    assert m % tm == 0 and n % tn == 0 and k % tk == 0 and k - kw < tk
    nk = k // tk
    return pl.pallas_call(
        functools.partial(_mm_resid_kernel, alpha=alpha, nk=nk, tk=tk, k_valid=kw),
        grid=(m // tm, n // tn, nk),
        in_specs=[
            pl.BlockSpec((tm, tk), lambda i, j, kk: (i, kk)),
            pl.BlockSpec((None, tk, tn), lambda i, j, kk: (layer, kk, j)),
            pl.BlockSpec((tm, tn), lambda i, j, kk: (i, j)),
        ],
        out_specs=pl.BlockSpec((tm, tn), lambda i, j, kk: (i, j)),
        out_shape=jax.ShapeDtypeStruct((m, n), F32),
        scratch_shapes=[pltpu.VMEM((tm, tn), F32)],
        compiler_params=_cparams(3),
        name=name,
    )(x, w, resid)


def _swiglu_kernel(x_ref, wg_ref, wu_ref, o_ref, *, n_valid_blocks):
    x = x_ref[...]
    g = jnp.dot(x, wg_ref[...].astype(BF16), preferred_element_type=F32)
    u = jnp.dot(x, wu_ref[...].astype(BF16), preferred_element_type=F32)
    h = g * _sigmoid(g) * u
    h = jnp.where(pl.program_id(1) < n_valid_blocks, h, 0.0)
    o_ref[...] = h.astype(o_ref.dtype)


def swiglu_up(x, wg, wu, layer, *, n_out, tm, tn, name):
    m, k = x.shape
    n = wg.shape[2]
    assert m % tm == 0 and n % tn == 0 and n_out % tn == 0 and n_out >= n
    nvb = n // tn
    wmap = lambda i, j: (layer, 0, jnp.minimum(j, nvb - 1))
    return pl.pallas_call(
        functools.partial(_swiglu_kernel, n_valid_blocks=nvb),
        grid=(m // tm, n_out // tn),
        in_specs=[
            pl.BlockSpec((tm, k), lambda i, j: (i, 0), pipeline_mode=pl.Buffered(1)),
            pl.BlockSpec((None, k, tn), wmap),
            pl.BlockSpec((None, k, tn), wmap),
        ],
        out_specs=pl.BlockSpec((tm, tn), lambda i, j: (i, j)),
        out_shape=jax.ShapeDtypeStruct((m, n_out), BF16),
        compiler_params=_cparams(2),
        name=name,
    )(x, wg, wu)


def _ln_kernel(y_ref, g_ref, b_ref, o32_ref, o16_ref):
    y = y_ref[...]
    mu = jnp.mean(y, axis=-1, keepdims=True)
    yc = y - mu
    var = jnp.mean(yc * yc, axis=-1, keepdims=True)
    x = yc * lax.rsqrt(var + LN_EPS) * g_ref[...] + b_ref[...]
    o32_ref[...] = x
    o16_ref[...] = x.astype(BF16)


def layer_norm(y, gain, bias, *, tm, name):
    m, d = y.shape
    assert m % tm == 0
    return pl.pallas_call(
        _ln_kernel,
        grid=(m // tm,),
        in_specs=[
            pl.BlockSpec((tm, d), lambda i: (i, 0)),
            pl.BlockSpec((1, d), lambda i: (0, 0)),
            pl.BlockSpec((1, d), lambda i: (0, 0)),
        ],
        out_specs=[
            pl.BlockSpec((tm, d), lambda i: (i, 0)),
            pl.BlockSpec((tm, d), lambda i: (i, 0)),
        ],
        out_shape=[jax.ShapeDtypeStruct((m, d), F32), jax.ShapeDtypeStruct((m, d), BF16)],
        compiler_params=_cparams(1),
        name=name,
    )(y, gain.reshape(1, d), bias.reshape(1, d))


ATTN_QB = 4 * CHUNK
ATTN_WIN = ATTN_QB + LEFT_CHUNKS * CHUNK
ATTN_NVAR = LEFT_CHUNKS * CHUNK // ATTN_QB + 1


def _attn_bias_tiles(rel_bias):
    n_heads = rel_bias.shape[0]
    period = ATTN_QB + ATTN_WIN
    m = jnp.arange(period)
    tiles = []
    for v in range(ATTN_NVAR):
        rel = v * ATTN_QB - jnp.where(m < ATTN_WIN, m, m - period)
        row = rel_bias.astype(F32)[:, jnp.clip(rel, -MAX_REL, MAX_REL) + MAX_REL]
        flat = jnp.tile(row, (1, ATTN_QB))[:, :ATTN_QB * (period - 1)]
        skew = flat.reshape(n_heads, ATTN_QB, period - 1)[:, :, :ATTN_WIN]
        tiles.append(skew)
    tiles = jnp.stack(tiles)
    i = jnp.arange(ATTN_QB)[None, :, None]
    j = jnp.arange(ATTN_WIN)[None, None, :]
    qpos = (jnp.arange(ATTN_NVAR) * ATTN_QB)[:, None, None] + i
    dchunk = qpos // CHUNK - j // CHUNK
    valid = (dchunk >= 0) & (dchunk <= LEFT_CHUNKS)
    return jnp.where(valid[:, None], tiles, NEG_BIG)


def _attn_kernel(q_ref, k_ref, v_ref, bias_ref, o_ref, *, hb, seq):
    scale = DH_ATTN ** -0.5
    nq = seq // ATTN_QB

    def body(qi, carry):
        qs = pl.multiple_of(qi * ATTN_QB, ATTN_QB)
        start = pl.multiple_of(jnp.maximum(qs - LEFT_CHUNKS * CHUNK, 0), ATTN_QB)
        var = jnp.minimum(qi, ATTN_NVAR - 1)
        heads = [slice(h * DH_ATTN, (h + 1) * DH_ATTN) for h in range(hb)]
        s_all = [lax.dot_general(q_ref[pl.ds(qs, ATTN_QB), c], k_ref[pl.ds(start, ATTN_WIN), c],
                                 (((1,), (1,)), ((), ())), preferred_element_type=F32)
                 for c in heads]
        p_all = []
        for h, s in enumerate(s_all):
            s = s * scale + bias_ref[var, h]
            m = jnp.max(s, axis=-1, keepdims=True)
            p = jnp.exp(s - m)
            l = jnp.sum(p, axis=-1, keepdims=True)
            p_all.append((p / l).astype(BF16))
        for c, p in zip(heads, p_all):
            o = jnp.dot(p, v_ref[pl.ds(start, ATTN_WIN), c], preferred_element_type=F32)
            o_ref[pl.ds(qs, ATTN_QB), c] = o.astype(o_ref.dtype)
        return carry

    lax.fori_loop(0, nq, body, 0)


def attention(qkv, rel_bias, n_heads, *, hb, out_cols, name):
    b, seq, _ = qkv.shape
    assert n_heads % hb == 0 and seq % ATTN_QB == 0 and seq >= ATTN_WIN
    ng = n_heads // hb
    wcols = hb * DH_ATTN
    bias = _attn_bias_tiles(rel_bias)
    kern = functools.partial(_attn_kernel, hb=hb, seq=seq)
    return pl.pallas_call(
        kern,
        grid=(b, ng),
        in_specs=[
            pl.BlockSpec((None, seq, wcols), lambda bi, g: (bi, 0, g)),
            pl.BlockSpec((None, seq, wcols), lambda bi, g: (bi, 0, ng + g)),
            pl.BlockSpec((None, seq, wcols), lambda bi, g: (bi, 0, 2 * ng + g)),
            pl.BlockSpec((ATTN_NVAR, hb, ATTN_QB, ATTN_WIN), lambda bi, g: (0, g, 0, 0)),
        ],
        out_specs=pl.BlockSpec((None, seq, wcols), lambda bi, g: (bi, 0, g)),
        out_shape=jax.ShapeDtypeStruct((b, seq, out_cols), BF16),
        compiler_params=_cparams(2),
        name=name,
    )(qkv, qkv, qkv, bias)


SSM_TILE_GROUPS = LANES // SSM_GROUP
SSM_TILE_STATES = SSM_TILE_GROUPS * SSM_STATE


def _ssm_tables(lam_re, lam_im, log_step, b_re, b_im, c_re, c_im, n_scan_steps):
    g, p = lam_re.shape
    hg = b_re.shape[-1]
    tc = SSM_BLOCK
    tg = SSM_TILE_GROUPS
    nt = g // tg
    eye = jnp.eye(tg, dtype=F32)
    lam = lax.complex(jnp.minimum(lam_re.astype(F32), -1e-4), lam_im.astype(F32))
    step = jnp.exp(log_step.astype(F32))[:, None]
    lam_bar = jnp.exp(lam * step)
    b_bar = ((lam_bar - 1.0) / lam)[..., None] * lax.complex(b_re.astype(F32), b_im.astype(F32))
    pw = [jnp.ones_like(lam_bar)]
    for _ in range(tc):
        pw.append(pw[-1] * lam_bar)
    pw = jnp.stack(pw)
    xb = pw[:tc, :, :, None] * b_bar[None]
    xb_re, xb_im = jnp.real(xb), jnp.imag(xb)
    cr, ci = c_re.astype(F32), c_im.astype(F32)
    kmat = (jnp.einsum('ghp,tgpi->tghi', cr, xb_re, precision=HI)
            - jnp.einsum('ghp,tgpi->tghi', ci, xb_im, precision=HI))
    kb = kmat.reshape(tc, nt, tg, hg, hg)
    k_tab = jnp.einsum('tlgoi,gk->lgitko', kb, eye).reshape(nt, tg * hg, tc * tg * hg)
    k_tab = jnp.pad(k_tab, ((0, 0), (0, 0), (tg * hg, 0)))
    def f_part(xpart):
        xr = xpart[::-1].reshape(tc, nt, tg, p, hg)
        return jnp.einsum('slgpi,gk->lsgikp', xr, eye).reshape(nt, tc, tg * hg, tg * p)
    f_tab = jnp.concatenate([f_part(xb_re), f_part(xb_im)], axis=-1)
    gmat = lax.complex(cr, ci)[None] * pw[1:, :, None, :]
    def e_part(gpart):
        gr = gpart.reshape(tc, nt, tg, hg, p)
        return jnp.einsum('tlgop,gk->lgptko', gr, eye).reshape(nt, tg * p, tc * tg * hg)
    e_tab = jnp.concatenate([e_part(jnp.real(gmat)), e_part(-jnp.imag(gmat))], axis=1)
    a = pw[tc]
    a1, a2 = [], []
    for _ in range(n_scan_steps):
        ar = jnp.real(a).reshape(nt, tg * p)
        ai = jnp.imag(a).reshape(nt, tg * p)
        a1.append(jnp.concatenate([ar, ar], axis=-1))
        a2.append(jnp.concatenate([-ai, ai], axis=-1))
        a = a * a
    pad = (-n_scan_steps) % 8
    a1 = jnp.pad(jnp.stack(a1, axis=1), ((0, 0), (0, pad), (0, 0)))
    a2 = jnp.pad(jnp.stack(a2, axis=1), ((0, 0), (0, pad), (0, 0)))
    return k_tab.astype(BF16), f_tab.astype(BF16), e_tab.astype(BF16), a1, a2


def _gelu_tanh(x):
    c = math.sqrt(2.0 / math.pi)
    return 0.5 * x * (1.0 + jnp.tanh(c * (x + 0.044715 * (x * x * x))))


def _ssm_kernel(u_ref, k_ref, f_ref, e_ref, a1_ref, a2_ref, d_ref, y_ref, xs_ref, *, nc, nsteps):
    tc = SSM_BLOCK
    ns = SSM_TILE_STATES

    def frames(s):
        return u_ref[pl.ds(s, nc, stride=tc), :]

    sloc = None
    for s in range(tc):
        xs = frames(s).astype(BF16)
        xs_ref[s] = xs
        part = jnp.dot(xs, f_ref[s], preferred_element_type=F32)
        sloc = part if sloc is None else sloc + part
    x = sloc
    for k in range(nsteps):
        d = 1 << k
        sh = jnp.concatenate([jnp.zeros((d, 2 * ns), F32), x[:nc - d]], axis=0)
        sw = pltpu.roll(sh, ns, 1)
        x = x + a1_ref[k:k + 1, :] * sh + a2_ref[k:k + 1, :] * sw
    sprev = jnp.concatenate([jnp.zeros((1, 2 * ns), F32), x[:nc - 1]], axis=0).astype(BF16)
    dskip = d_ref[...]
    for t in range(0, tc, 2):
        acc = jnp.dot(sprev, e_ref[:, t * LANES:(t + 2) * LANES], preferred_element_type=F32)
        for s in range(t + 2):
            c0 = (t - s + 1) * LANES
            acc = acc + jnp.dot(xs_ref[s], k_ref[:, c0:c0 + 2 * LANES], preferred_element_type=F32)
        for j in range(2):
            y = acc[:, j * LANES:(j + 1) * LANES] + dskip * frames(t + j)
            y_ref[pl.ds(t + j, nc, stride=tc), :] = _gelu_tanh(y)


def ssm_gelu(u, lam_re, lam_im, log_step, b_re, b_im, c_re, c_im, d_skip, *, name):
    b, seq, ds = u.shape
    g = lam_re.shape[0]
    tc = SSM_BLOCK
    nc = seq // tc
    assert seq % tc == 0 and ds == g * SSM_GROUP and ds % LANES == 0 and nc & (nc - 1) == 0
    assert lam_re.shape[1] == SSM_STATE
    nt = ds // LANES
    nsteps = nc.bit_length() - 1
    k_tab, f_tab, e_tab, a1, a2 = _ssm_tables(lam_re, lam_im, log_step, b_re, b_im, c_re, c_im, nsteps)
    ns2 = 2 * SSM_TILE_STATES
    sp = a1.shape[1]
    kern = functools.partial(_ssm_kernel, nc=nc, nsteps=nsteps)
    return pl.pallas_call(
        kern,
        grid=(nt, b),
        in_specs=[
            pl.BlockSpec((None, seq, LANES), lambda l, bi: (bi, 0, l)),
            pl.BlockSpec((None, LANES, (tc + 1) * LANES), lambda l, bi: (l, 0, 0)),
            pl.BlockSpec((None, tc, LANES, ns2), lambda l, bi: (l, 0, 0, 0)),
            pl.BlockSpec((None, ns2, tc * LANES), lambda l, bi: (l, 0, 0)),
            pl.BlockSpec((None, sp, ns2), lambda l, bi: (l, 0, 0)),
            pl.BlockSpec((None, sp, ns2), lambda l, bi: (l, 0, 0)),
            pl.BlockSpec((1, LANES), lambda l, bi: (0, l)),
        ],
        out_specs=pl.BlockSpec((None, seq, LANES), lambda l, bi: (bi, 0, l)),
        out_shape=jax.ShapeDtypeStruct((b, seq, ds), F32),
        scratch_shapes=[pltpu.VMEM((tc, nc, LANES), BF16)],
        compiler_params=_cparams(2),
        name=name,
    )(u, k_tab, f_tab, e_tab, a1, a2, d_skip.reshape(1, ds).astype(F32))


def _glu_kernel(y_ref, w_ref, b_ref, dst_ref, o_ref):
    del dst_ref
    yb = y_ref[...]
    z = jnp.dot(yb.astype(BF16), w_ref[...], preferred_element_type=F32) + b_ref[...]
    o_ref[...] = (yb * _sigmoid(z)).astype(o_ref.dtype)


def glu(y, w_glu, b_glu, dst, *, tm, name):
    m, d = y.shape
    assert m % tm == 0 and dst.shape[0] == m and dst.shape[1] % d == 0 and dst.dtype == BF16
    col_blk = dst.shape[1] // d - 1
    return pl.pallas_call(
        _glu_kernel,
        grid=(m // tm,),
        in_specs=[
            pl.BlockSpec((tm, d), lambda i: (i, 0)),
            pl.BlockSpec((d, d), lambda i: (0, 0)),
            pl.BlockSpec((1, d), lambda i: (0, 0)),
            pl.BlockSpec(memory_space=pl.ANY),
        ],
        out_specs=pl.BlockSpec((tm, d), lambda i: (i, col_blk)),
        out_shape=jax.ShapeDtypeStruct(dst.shape, BF16),
        input_output_aliases={3: 0},
        compiler_params=_cparams(1),
        name=name,
    )(y, w_glu, b_glu.reshape(1, d).astype(F32), dst)


def _softplus(x):
    return jnp.maximum(x, 0.0) + jnp.log(1.0 + jnp.exp(-jnp.abs(x)))


def _cumsum_rows(x, n):
    rows = lax.broadcasted_iota(jnp.int32, x.shape, 0)
    d = 1
    while d < n:
        x = x + jnp.where(rows >= d, pltpu.roll(x, d, 0), 0.0)
        d *= 2
    return x


def _cumsum_lane_segments(x, seg):
    pos = lax.broadcasted_iota(jnp.int32, x.shape, 1) & (seg - 1)
    d = 1
    while d < seg:
        x = x + jnp.where(pos >= d, pltpu.roll(x, d, 1), 0.0)
        d *= 2
    return x


DELTA_PACK = 4
DELTA_ROWS = DELTA_PACK * CHUNK
DELTA_LEVELS = CHUNK.bit_length() - 1


def _delta_level_map():
    import numpy as np
    i = np.arange(DELTA_ROWS)[:, None]
    j = np.arange(DELTA_ROWS)[None, :]
    x = i ^ j
    lev = np.zeros_like(x)
    for a in range(1, 8):
        lev += (x >> a) != 0
    out = np.where((i > j) & ((i // CHUNK) == (j // CHUNK)), lev, 99)
    out = np.where(i == j, -1, out)
    return out.astype(np.int32)


def _delta_kernel(q_ref, k_ref, v_ref, gate_ref, bgc_ref, gr_ref, cwq_ref, cwk_ref, cwv_ref,
                  pc_ref, pr_ref, nw_ref, lev_ref, o_ref, s_ref, halo_ref, *, hb, tb):
    c = CHUNK
    dk = DK_DELTA
    pk = DELTA_PACK
    rows = DELTA_ROWS
    sub = hb // pk
    scale = dk ** -0.5
    n_chunks = tb // c
    ti = pl.program_id(2)

    @pl.when(ti == 0)
    def _():
        s_ref[...] = jnp.zeros_like(s_ref)
        halo_ref[...] = jnp.zeros_like(halo_ref)

    alog_c = pc_ref[0:1, :]
    dtb_c = pc_ref[1:2, :]
    alog_r = pr_ref[0:sub, :]
    dtb_r = pr_ref[sub:2 * sub, :]
    nw = nw_ref[...]

    def conv_silu(win, cw_ref):
        acc = (cw_ref[3:4, :] * win[8:8 + c] + cw_ref[2:3, :] * win[7:7 + c]
               + cw_ref[1:2, :] * win[6:6 + c] + cw_ref[0:1, :] * win[5:5 + c])
        return acc * _sigmoid(acc)

    def stack_heads(x, s):
        return jnp.concatenate(
            [x[:, (pk * s + j) * dk:(pk * s + j + 1) * dk] for j in range(pk)], axis=0)

    def stack_cols(x, s, off):
        return jnp.concatenate(
            [x[:, off + pk * s + j:off + pk * s + j + 1] for j in range(pk)], axis=0)

    def chunk_body(ch, tails):
        tq, tk, tv = tails
        r0 = pl.multiple_of(ch * c, c)
        curq = q_ref[pl.ds(r0, c), :].astype(F32)
        curk = k_ref[pl.ds(r0, c), :].astype(F32)
        curv = v_ref[pl.ds(r0, c), :].astype(F32)
        qa = conv_silu(jnp.concatenate([tq, curq], axis=0), cwq_ref)
        ka = conv_silu(jnp.concatenate([tk, curk], axis=0), cwk_ref)
        va = conv_silu(jnp.concatenate([tv, curv], axis=0), cwv_ref)
        gate = gate_ref[pl.ds(r0, c), :].astype(F32)

        bg = bgc_ref[pl.ds(r0, c), :]
        beta_all = _sigmoid(bg)
        g_all = -jnp.exp(alog_c) * _softplus(bg + dtb_c)
        gc_all = _cumsum_rows(g_all, c)
        gl_all = jnp.broadcast_to(gc_all[c - 1:c, :], gc_all.shape)
        g_row = -jnp.exp(alog_r) * _softplus(gr_ref[ch] + dtb_r)
        gcr_all = _cumsum_lane_segments(g_row, c)

        lev = lev_ref[...]
        subs = []
        for s in range(sub):
            q = stack_heads(qa, s)
            k = stack_heads(ka, s)
            v = stack_heads(va, s)
            q = q * lax.rsqrt(jnp.sum(q * q, axis=-1, keepdims=True) + NORM_EPS)
            k = k * lax.rsqrt(jnp.sum(k * k, axis=-1, keepdims=True) + NORM_EPS)
            qs = q * scale
            beta = stack_cols(beta_all, s, 0)
            gc_col = stack_cols(gc_all, s, hb)
            gl_col = stack_cols(gl_all, s, hb)
            gc_row = gcr_all[s:s + 1, :]
            egc = jnp.exp(gc_col)
            kb = k * beta
            lhs = jnp.concatenate([kb, qs], axis=0).astype(BF16)
            kq = lax.dot_general(lhs, k.astype(BF16), (((1,), (1,)), ((), ())),
                                 preferred_element_type=F32)
            dec = jnp.exp(gc_col - gc_row)
            nm = kq[:rows] * dec
            subs.append(dict(
                nm=nm, qg=qs * egc,
                attn=jnp.where(lev < DELTA_LEVELS, kq[rows:] * dec, 0.0),
                x=jnp.concatenate([v * beta, kb * egc], axis=1),
                kd=(k * jnp.exp(gl_col - gc_col)).astype(BF16),
                t=jnp.where(lev == 0, -nm, jnp.where(lev == -1, 1.0, 0.0))))
        for a in range(1, DELTA_LEVELS):
            for st in subs:
                cm = jnp.where(lev == a, st["nm"], 0.0).astype(BF16)
                tb16 = st["t"].astype(BF16)
                pm = jnp.dot(tb16, cm, preferred_element_type=F32)
                st["t"] = st["t"] - jnp.dot(pm.astype(BF16), tb16, preferred_element_type=F32)
        for st in subs:
            t_off = jnp.where(lev == -1, 0.0, st["t"]).astype(BF16)
            x = st["x"]
            st["x"] = x + jnp.dot(t_off, x.astype(BF16), preferred_element_type=F32)
        for s, st in enumerate(subs):
            w_c = st["x"][:, dk:]
            qg = st["qg"]
            ws_parts, qs_parts = [], []
            for j in range(pk):
                h = pk * s + j
                rs = slice(j * c, (j + 1) * c)
                lhs2 = jnp.concatenate([w_c[rs], qg[rs]], axis=0).astype(BF16)
                ws = jnp.dot(lhs2, s_ref[h].astype(BF16), preferred_element_type=F32)
                ws_parts.append(ws[:c])
                qs_parts.append(ws[c:])
            st["v_new"] = st["x"][:, :dk] - jnp.concatenate(ws_parts, axis=0)
            st["qs"] = jnp.concatenate(qs_parts, axis=0)
        for st in subs:
            st["out"] = st["qs"] + jnp.dot(st["attn"].astype(BF16), st["v_new"].astype(BF16),
                                           preferred_element_type=F32)
        ri = lax.broadcasted_iota(jnp.int32, (rows, pk * dk), 0)
        cj = lax.broadcasted_iota(jnp.int32, (rows, pk * dk), 1)
        head_block = (ri // c) == (cj // dk)
        for st in subs:
            vbd = jnp.where(head_block, jnp.concatenate([st["v_new"]] * pk, axis=1), 0.0)
            st["upd"] = lax.dot_general(st["kd"], vbd.astype(BF16), (((0,), (0,)), ((), ())),
                                        preferred_element_type=F32)
        for s, st in enumerate(subs):
            out = st["out"]
            o = out * lax.rsqrt(jnp.mean(out * out, axis=-1, keepdims=True) + NORM_EPS) * nw
            gt = stack_heads(gate, s)
            res = (o * (gt * _sigmoid(gt))).astype(o_ref.dtype)
            for j in range(pk):
                h = pk * s + j
                e_last = jnp.exp(gc_all[c - 1:c, hb + h:hb + h + 1])
                s_ref[h] = s_ref[h] * e_last + st["upd"][:, j * dk:(j + 1) * dk]
                o_ref[pl.ds(r0, c), h * dk:(h + 1) * dk] = res[j * c:(j + 1) * c]

        return (curq[c - 8:], curk[c - 8:], curv[c - 8:])

    tails0 = (halo_ref[0], halo_ref[1], halo_ref[2])
    tq, tk, tv = lax.fori_loop(0, n_chunks, chunk_body, tails0)
    halo_ref[0] = tq
    halo_ref[1] = tk
    halo_ref[2] = tv


def gated_deltanet(proj, braw, conv_w, a_log, dt_bias, norm_w, n_heads, *, hb, tb, name):
    b, seq, _ = proj.shape
    dk = DK_DELTA
    c = CHUNK
    pk = DELTA_PACK
    assert n_heads % hb == 0 and hb % pk == 0 and seq % tb == 0 and tb % c == 0
    ng = n_heads // hb
    sub = hb // pk
    w = hb * dk
    nct = seq // c
    br = braw.reshape(b, seq, 2, ng, hb)
    bgc = jnp.transpose(br, (0, 3, 1, 2, 4)).reshape(b, ng, seq, 2 * hb)
    araw = braw[..., n_heads:].reshape(b, nct, c, ng, sub, pk)
    gr = jnp.transpose(araw, (0, 3, 1, 4, 5, 2)).reshape(b, ng, nct, sub, pk * c)
    al = a_log.astype(F32).reshape(ng, hb)
    dt = dt_bias.astype(F32).reshape(ng, hb)
    zeros = jnp.zeros((ng, hb), F32)
    pc = jnp.stack([jnp.concatenate([zeros, al], axis=1),
                    jnp.concatenate([zeros, dt], axis=1)], axis=1)
    rep = lambda p: jnp.repeat(p.reshape(ng, sub, pk), c, axis=-1)
    pr = jnp.concatenate([rep(al), rep(dt)], axis=1)
    cw = conv_w.astype(F32)
    lev = jnp.asarray(_delta_level_map())
    rows = pk * c
    kern = functools.partial(_delta_kernel, hb=hb, tb=tb)
    return pl.pallas_call(
        kern,
        grid=(b, ng, seq // tb),
        in_specs=[
            pl.BlockSpec((None, tb, w), lambda bi, g, t: (bi, t, g)),
            pl.BlockSpec((None, tb, w), lambda bi, g, t: (bi, t, ng + g)),
            pl.BlockSpec((None, tb, w), lambda bi, g, t: (bi, t, 2 * ng + g)),
            pl.BlockSpec((None, tb, w), lambda bi, g, t: (bi, t, 3 * ng + g)),
            pl.BlockSpec((None, None, tb, 2 * hb), lambda bi, g, t: (bi, g, t, 0)),
            pl.BlockSpec((None, None, tb // c, sub, rows), lambda bi, g, t: (bi, g, t, 0, 0)),
            pl.BlockSpec((CONV_K, w), lambda bi, g, t: (0, g)),
            pl.BlockSpec((CONV_K, w), lambda bi, g, t: (0, ng + g)),
            pl.BlockSpec((CONV_K, w), lambda bi, g, t: (0, 2 * ng + g)),
            pl.BlockSpec((None, 2, 2 * hb), lambda bi, g, t: (g, 0, 0)),
            pl.BlockSpec((None, 2 * sub, rows), lambda bi, g, t: (g, 0, 0)),
            pl.BlockSpec((1, dk), lambda bi, g, t: (0, 0)),
            pl.BlockSpec((rows, rows), lambda bi, g, t: (0, 0)),
        ],
        out_specs=pl.BlockSpec((None, tb, w), lambda bi, g, t: (bi, t, g)),
        out_shape=jax.ShapeDtypeStruct((b, seq, n_heads * dk), BF16),
        scratch_shapes=[pltpu.VMEM((hb, dk, dk), F32), pltpu.VMEM((3, 8, w), F32)],
        compiler_params=_cparams(3),
        name=name,
    )(proj, proj, proj, proj, bgc, gr, cw, cw, cw, pc, pr, norm_w.astype(F32).reshape(1, dk), lev)


FFN_TILE = 1024


def _ffn(x32, x16, w_gate, w_up, w_down, gain, bias, layer):
    dff = w_gate.shape[2]
    dff_pad = -(-dff // FFN_TILE) * FFN_TILE
    h = swiglu_up(x16, w_gate, w_up, layer, n_out=dff_pad, tm=2048, tn=256, name=f"ffn_up_{layer}")
    y = matmul_resid(h, w_down, layer, x32, ALPHA, tm=1024, tn=1024, tk=FFN_TILE,
                     name=f"ffn_down_{layer}")
    return layer_norm(y, gain, bias, tm=256, name=f"ffn_ln_{layer}")


def kernel(x, ab_w_in, ab_rel_bias, ab_lam_re, ab_lam_im, ab_log_step, ab_b_re, ab_b_im, ab_c_re, ab_c_im, ab_d_skip, ab_w_glu, ab_b_glu, ab_w_out, c_w_in, c_conv, c_a_log, c_dt_bias, c_norm_w, c_w_out, ffn_w_gate, ffn_w_up, ffn_w_down, ln_gain, ln_bias):
    b, seq, d = x.shape
    t = b * seq
    x32 = x.reshape(t, d).astype(F32)
    x16 = x32.astype(BF16)

    n_heads_a = ab_rel_bias.shape[1]
    d_attn = n_heads_a * DH_ATTN
    d_ssm = ab_w_in.shape[2] - 3 * d_attn
    qkv = matmul(x16, ab_w_in, 0, BF16, col0=0, n=3 * d_attn, tm=2048, tn=512, name="ab_in_qkv")
    u = matmul(x16, ab_w_in, 0, F32, col0=3 * d_attn, n=d_ssm, tm=2048, tn=512, name="ab_in_u")
    y_ab = attention(qkv.reshape(b, seq, 3 * d_attn), ab_rel_bias[0], n_heads_a, hb=2,
                     out_cols=d_attn + d_ssm, name="ab_attn")
    y_s = ssm_gelu(u.reshape(b, seq, d_ssm), ab_lam_re[0], ab_lam_im[0], ab_log_step[0], ab_b_re[0],
                   ab_b_im[0], ab_c_re[0], ab_c_im[0], ab_d_skip[0].reshape(-1), name="ab_ssm")
    y_ab = glu(y_s.reshape(t, d_ssm), ab_w_glu[0].astype(BF16), ab_b_glu[0],
               y_ab.reshape(t, d_attn + d_ssm), tm=512, name="ab_glu")
    y = matmul_resid(y_ab, ab_w_out, 0, x32, ALPHA, tm=1024, tn=1024, tk=1024, name="ab_out")
    x32, x16 = layer_norm(y, ln_gain[0, 0], ln_bias[0, 0], tm=256, name="ab_ln")
    x32, x16 = _ffn(x32, x16, ffn_w_gate, ffn_w_up, ffn_w_down, ln_gain[0, 1], ln_bias[0, 1], 0)

    n_heads_c = c_a_log.shape[1]
    d_delta = n_heads_c * DK_DELTA
    proj = matmul(x16, c_w_in, 0, BF16, col0=0, n=4 * d_delta, tm=2048, tn=512, name="c_in")
    braw = matmul(x16, c_w_in[:, :, 4 * d_delta:], 0, F32, col0=0, n=2 * n_heads_c, tm=2048,
                  tn=2 * n_heads_c, name="c_in_small")
    o = gated_deltanet(proj.reshape(b, seq, 4 * d_delta), braw.reshape(b, seq, 2 * n_heads_c),
                       c_conv[0], c_a_log[0], c_dt_bias[0], c_norm_w[0], n_heads_c,
                       hb=16, tb=512, name="c_delta")
    y = matmul_resid(o.reshape(t, d_delta), c_w_out, 0, x32, ALPHA, tm=1024, tn=1024, tk=1024,
                     name="c_out")
    x32, x16 = layer_norm(y, ln_gain[1, 0], ln_bias[1, 0], tm=256, name="c_ln")
    x32, x16 = _ffn(x32, x16, ffn_w_gate, ffn_w_up, ffn_w_down, ln_gain[1, 1], ln_bias[1, 1], 1)
    return x32.reshape(b, seq, d).astype(x.dtype)
```

```python
import functools
import math

import jax
import jax.numpy as jnp
from jax import lax
from jax.experimental import pallas as pl
from jax.experimental.pallas import tpu as pltpu

F32 = jnp.float32
BF16 = jnp.bfloat16

CHUNK = 64
LEFT_CHUNKS = 8
DH_ATTN = 128
MAX_REL = 256
SSM_GROUP = 16
SSM_STATE = 64
DK_DELTA = 128
CONV_K = 4
DEPTH = 2
ALPHA = (2.0 * DEPTH) ** 0.25
LN_EPS = 1e-5
NORM_EPS = 1e-6
NEG_BIG = -1e30

V7X_VMEM_LIMIT_BYTES = 56 * 1024 * 1024
LANES = 128
SSM_BLOCK = 16
HI = lax.Precision.HIGHEST


def _cparams(n_axes, flags=None):
    return pltpu.CompilerParams(
        dimension_semantics=("arbitrary",) * n_axes,
        vmem_limit_bytes=V7X_VMEM_LIMIT_BYTES,
        flags=flags,
    )


def _sigmoid(x):
    return 1.0 / (1.0 + jnp.exp(-x))


def _mm_kernel(x_ref, w_ref, *rest, alpha, transposed):
    o_ref = rest[-1]
    w = w_ref[...].astype(BF16)
    dims = (((1,), (1,)), ((), ())) if transposed else (((1,), (0,)), ((), ()))
    y = lax.dot_general(x_ref[...], w, dims, preferred_element_type=F32)
    if len(rest) == 2:
        y = alpha * rest[0][...] + y
    o_ref[...] = y.astype(o_ref.dtype)


def matmul(x, w, layer, out_dtype, *, col0, n, tm, tn, name, transposed=False, resid=None, alpha=None):
    m, k = x.shape
    assert m % tm == 0 and n % tn == 0 and col0 % tn == 0
    j0 = col0 // tn
    if transposed:
        w_spec = pl.BlockSpec((None, tn, k), lambda i, j: (layer, j0 + j, 0))
    else:
        w_spec = pl.BlockSpec((None, k, tn), lambda i, j: (layer, 0, j0 + j))
    in_specs = [pl.BlockSpec((tm, k), lambda i, j: (i, 0), pipeline_mode=pl.Buffered(1)), w_spec]
    args = [x, w]
    if resid is not None:
        in_specs.append(pl.BlockSpec((tm, tn), lambda i, j: (i, j)))
        args.append(resid)
    return pl.pallas_call(
        functools.partial(_mm_kernel, alpha=alpha, transposed=transposed),
        grid=(m // tm, n // tn),
        in_specs=in_specs,
        out_specs=pl.BlockSpec((tm, tn), lambda i, j: (i, j)),
        out_shape=jax.ShapeDtypeStruct((m, n), out_dtype),
        compiler_params=_cparams(2),
        name=name,
    )(*args)


def _swiglu_kernel(x_ref, wg_ref, wu_ref, o_ref, *, n_valid_blocks):
    x = x_ref[...]
    g = jnp.dot(x, wg_ref[...].astype(BF16), preferred_element_type=F32)
    u = jnp.dot(x, wu_ref[...].astype(BF16), preferred_element_type=F32)
    h = g * _sigmoid(g) * u
    h = jnp.where(pl.program_id(1) < n_valid_blocks, h, 0.0)
    o_ref[...] = h.astype(o_ref.dtype)


def swiglu_up(x, wg, wu, layer, *, n_out, tm, tn, name):
    m, k = x.shape
    n = wg.shape[2]
    assert m % tm == 0 and n % tn == 0 and n_out % tn == 0 and n_out >= n
    nvb = n // tn
    wmap = lambda i, j: (layer, 0, jnp.minimum(j, nvb - 1))
    return pl.pallas_call(
        functools.partial(_swiglu_kernel, n_valid_blocks=nvb),
        grid=(m // tm, n_out // tn),
        in_specs=[
            pl.BlockSpec((tm, k), lambda i, j: (i, 0), pipeline_mode=pl.Buffered(1)),
            pl.BlockSpec((None, k, tn), wmap),
            pl.BlockSpec((None, k, tn), wmap),
        ],
        out_specs=pl.BlockSpec((tm, tn), lambda i, j: (i, j)),
        out_shape=jax.ShapeDtypeStruct((m, n_out), BF16),
        compiler_params=_cparams(2),
        name=name,
    )(x, wg, wu)


def _ln_kernel(y_ref, g_ref, b_ref, o32_ref, o16_ref):
    y = y_ref[...]
    mu = jnp.mean(y, axis=-1, keepdims=True)
    yc = y - mu
    var = jnp.mean(yc * yc, axis=-1, keepdims=True)
    x = yc * lax.rsqrt(var + LN_EPS) * g_ref[...] + b_ref[...]
    o32_ref[...] = x
    o16_ref[...] = x.astype(BF16)


def layer_norm(y, gain, bias, *, tm, name):
    m, d = y.shape
    assert m % tm == 0
    return pl.pallas_call(
        _ln_kernel,
        grid=(m // tm,),
        in_specs=[
            pl.BlockSpec((tm, d), lambda i: (i, 0)),
            pl.BlockSpec((1, d), lambda i: (0, 0)),
            pl.BlockSpec((1, d), lambda i: (0, 0)),
        ],
        out_specs=[
            pl.BlockSpec((tm, d), lambda i: (i, 0)),
            pl.BlockSpec((tm, d), lambda i: (i, 0)),
        ],
        out_shape=[jax.ShapeDtypeStruct((m, d), F32), jax.ShapeDtypeStruct((m, d), BF16)],
        compiler_params=_cparams(1),
        name=name,
    )(y, gain.reshape(1, d), bias.reshape(1, d))


ATTN_QB = 4 * CHUNK
ATTN_WIN = ATTN_QB + LEFT_CHUNKS * CHUNK
ATTN_NVAR = LEFT_CHUNKS * CHUNK // ATTN_QB + 1


def _attn_bias_tiles(rel_bias):
    n_heads = rel_bias.shape[0]
    period = ATTN_QB + ATTN_WIN
    m = jnp.arange(period)
    tiles = []
    for v in range(ATTN_NVAR):
        rel = v * ATTN_QB - jnp.where(m < ATTN_WIN, m, m - period)
        row = rel_bias.astype(F32)[:, jnp.clip(rel, -MAX_REL, MAX_REL) + MAX_REL]
        flat = jnp.tile(row, (1, ATTN_QB))[:, :ATTN_QB * (period - 1)]
        skew = flat.reshape(n_heads, ATTN_QB, period - 1)[:, :, :ATTN_WIN]
        tiles.append(skew)
    tiles = jnp.stack(tiles)
    i = jnp.arange(ATTN_QB)[None, :, None]
    j = jnp.arange(ATTN_WIN)[None, None, :]
    qpos = (jnp.arange(ATTN_NVAR) * ATTN_QB)[:, None, None] + i
    dchunk = qpos // CHUNK - j // CHUNK
    valid = (dchunk >= 0) & (dchunk <= LEFT_CHUNKS)
    return jnp.where(valid[:, None], tiles, NEG_BIG)


def _attn_kernel(q_ref, k_ref, v_ref, bias_ref, o_ref, *, hb, seq):
    scale = DH_ATTN ** -0.5
    nq = seq // ATTN_QB

    def body(qi, carry):
        qs = pl.multiple_of(qi * ATTN_QB, ATTN_QB)
        start = pl.multiple_of(jnp.maximum(qs - LEFT_CHUNKS * CHUNK, 0), ATTN_QB)
        var = jnp.minimum(qi, ATTN_NVAR - 1)
        heads = [slice(h * DH_ATTN, (h + 1) * DH_ATTN) for h in range(hb)]
        s_all = [lax.dot_general(q_ref[pl.ds(qs, ATTN_QB), c], k_ref[pl.ds(start, ATTN_WIN), c],
                                 (((1,), (1,)), ((), ())), preferred_element_type=F32)
                 for c in heads]
        p_all = []
        for h, s in enumerate(s_all):
            s = s * scale + bias_ref[var, h]
            m = jnp.max(s, axis=-1, keepdims=True)
            p = jnp.exp(s - m)
            l = jnp.sum(p, axis=-1, keepdims=True)
            p_all.append((p / l).astype(BF16))
        for c, p in zip(heads, p_all):
            o = jnp.dot(p, v_ref[pl.ds(start, ATTN_WIN), c], preferred_element_type=F32)
            o_ref[pl.ds(qs, ATTN_QB), c] = o.astype(o_ref.dtype)
        return carry

    lax.fori_loop(0, nq, body, 0)


def attention(qkv, rel_bias, n_heads, *, hb, out_cols, name):
    b, seq, _ = qkv.shape
    assert n_heads % hb == 0 and seq % ATTN_QB == 0 and seq >= ATTN_WIN
    ng = n_heads // hb
    wcols = hb * DH_ATTN
    bias = _attn_bias_tiles(rel_bias)
    kern = functools.partial(_attn_kernel, hb=hb, seq=seq)
    return pl.pallas_call(
        kern,
        grid=(b, ng),
        in_specs=[
            pl.BlockSpec((None, seq, wcols), lambda bi, g: (bi, 0, g)),
            pl.BlockSpec((None, seq, wcols), lambda bi, g: (bi, 0, ng + g)),
            pl.BlockSpec((None, seq, wcols), lambda bi, g: (bi, 0, 2 * ng + g)),
            pl.BlockSpec((ATTN_NVAR, hb, ATTN_QB, ATTN_WIN), lambda bi, g: (0, g, 0, 0)),
        ],
        out_specs=pl.BlockSpec((None, seq, wcols), lambda bi, g: (bi, 0, g)),
        out_shape=jax.ShapeDtypeStruct((b, seq, out_cols), BF16),
        compiler_params=_cparams(2),
        name=name,
    )(qkv, qkv, qkv, bias)


SSM_TILE_GROUPS = LANES // SSM_GROUP
SSM_TILE_STATES = SSM_TILE_GROUPS * SSM_STATE


def _ssm_tables(lam_re, lam_im, log_step, b_re, b_im, c_re, c_im, n_scan_steps):
    g, p = lam_re.shape
    hg = b_re.shape[-1]
    tc = SSM_BLOCK
    tg = SSM_TILE_GROUPS
    nt = g // tg
    lam = lax.complex(jnp.minimum(lam_re.astype(F32), -1e-4), lam_im.astype(F32))
    step = jnp.exp(log_step.astype(F32))[:, None]
    lam_bar = jnp.exp(lam * step)
    b_bar = ((lam_bar - 1.0) / lam)[..., None] * lax.complex(b_re.astype(F32), b_im.astype(F32))
    pw = [jnp.ones_like(lam_bar)]
    for _ in range(tc):
        pw.append(pw[-1] * lam_bar)
    pw = jnp.stack(pw)
    xb = pw[:tc, :, :, None] * b_bar[None]
    xb_re, xb_im = jnp.real(xb), jnp.imag(xb)
    cr, ci = c_re.astype(F32), c_im.astype(F32)
    kmat = (jnp.einsum('ghp,tgpi->tghi', cr, xb_re, precision=HI)
            - jnp.einsum('ghp,tgpi->tghi', ci, xb_im, precision=HI))
    def diag_mask(rows_per_group, cols_per_group, col_groups_stride):
        r = jnp.arange(tg * rows_per_group)[:, None] // rows_per_group
        c = (jnp.arange(col_groups_stride)[None, :] // cols_per_group) % tg
        return r == c
    kc = jnp.transpose(kmat.reshape(tc, nt, tg, hg, hg), (1, 2, 4, 0, 3))
    kc = jnp.tile(kc.reshape(nt, tg * hg, tc, hg), (1, 1, 1, tg)).reshape(nt, tg * hg, tc * tg * hg)
    k_tab = jnp.where(diag_mask(hg, hg, tc * tg * hg), kc, 0.0).astype(BF16)
    k_tab = jnp.pad(k_tab, ((0, 0), (0, 0), (tg * hg, 0)))
    def f_part(xpart):
        xr = jnp.transpose(xpart[::-1].reshape(tc, nt, tg, p, hg), (1, 0, 2, 4, 3))
        xr = jnp.tile(xr.reshape(nt, tc, tg * hg, p), (1, 1, 1, tg))
        return jnp.where(diag_mask(hg, p, tg * p), xr, 0.0).astype(BF16)
    f_tab = jnp.concatenate([f_part(xb_re), f_part(xb_im)], axis=-1)
    gmat = lax.complex(cr, ci)[None] * pw[1:, :, None, :]
    def e_part(gpart):
        gr = jnp.transpose(gpart.reshape(tc, nt, tg, hg, p), (1, 2, 4, 0, 3))
        gr = jnp.tile(gr.reshape(nt, tg * p, tc, hg), (1, 1, 1, tg)).reshape(nt, tg * p, tc * tg * hg)
        return jnp.where(diag_mask(p, hg, tc * tg * hg), gr, 0.0).astype(BF16)
    e_tab = jnp.concatenate([e_part(jnp.real(gmat)), e_part(-jnp.imag(gmat))], axis=1)
    a = pw[tc]
    a1, a2 = [], []
    for _ in range(n_scan_steps):
        ar = jnp.real(a).reshape(nt, tg * p)
        ai = jnp.imag(a).reshape(nt, tg * p)
        a1.append(jnp.concatenate([ar, ar], axis=-1))
        a2.append(jnp.concatenate([-ai, ai], axis=-1))
        a = a * a
    pad = (-n_scan_steps) % 8
    a1 = jnp.pad(jnp.stack(a1, axis=1), ((0, 0), (0, pad), (0, 0)))
    a2 = jnp.pad(jnp.stack(a2, axis=1), ((0, 0), (0, pad), (0, 0)))
    return k_tab, f_tab, e_tab, a1, a2


def _gelu_tanh(x):
    c = math.sqrt(2.0 / math.pi)
    return 0.5 * x * (1.0 + jnp.tanh(c * (x + 0.044715 * (x * x * x))))


def _ssm_kernel(u_ref, k_ref, f_ref, e_ref, a1_ref, a2_ref, d_ref, y_ref, xs_ref, *, nc, nsteps):
    tc = SSM_BLOCK
    ns = SSM_TILE_STATES

    def frames(s):
        return u_ref[pl.ds(s, nc, stride=tc), :]

    sloc = None
    for s in range(tc):
        xs = frames(s).astype(BF16)
        xs_ref[s] = xs
        part = jnp.dot(xs, f_ref[s], preferred_element_type=F32)
        sloc = part if sloc is None else sloc + part
    x = sloc
    for k in range(nsteps):
        d = 1 << k
        sh = jnp.concatenate([jnp.zeros((d, 2 * ns), F32), x[:nc - d]], axis=0)
        sw = pltpu.roll(sh, ns, 1)
        x = x + a1_ref[k:k + 1, :] * sh + a2_ref[k:k + 1, :] * sw
    sprev = jnp.concatenate([jnp.zeros((1, 2 * ns), F32), x[:nc - 1]], axis=0).astype(BF16)
    dskip = d_ref[...]
    for t in range(0, tc, 2):
        acc = jnp.dot(sprev, e_ref[:, t * LANES:(t + 2) * LANES], preferred_element_type=F32)
        for s in range(t + 2):
            c0 = (t - s + 1) * LANES
            acc = acc + jnp.dot(xs_ref[s], k_ref[:, c0:c0 + 2 * LANES], preferred_element_type=F32)
        for j in range(2):
            y = acc[:, j * LANES:(j + 1) * LANES] + dskip * frames(t + j)
            y_ref[pl.ds(t + j, nc, stride=tc), :] = _gelu_tanh(y)


def ssm_gelu(u, lam_re, lam_im, log_step, b_re, b_im, c_re, c_im, d_skip, *, name):
    b, seq, ds = u.shape
    g = lam_re.shape[0]
    tc = SSM_BLOCK
    nc = seq // tc
    assert seq % tc == 0 and ds == g * SSM_GROUP and ds % LANES == 0 and nc & (nc - 1) == 0
    assert lam_re.shape[1] == SSM_STATE
    nt = ds // LANES
    nsteps = nc.bit_length() - 1
    k_tab, f_tab, e_tab, a1, a2 = _ssm_tables(lam_re, lam_im, log_step, b_re, b_im, c_re, c_im, nsteps)
    ns2 = 2 * SSM_TILE_STATES
    sp = a1.shape[1]
    kern = functools.partial(_ssm_kernel, nc=nc, nsteps=nsteps)
    return pl.pallas_call(
        kern,
        grid=(nt, b),
        in_specs=[
            pl.BlockSpec((None, seq, LANES), lambda l, bi: (bi, 0, l)),
            pl.BlockSpec((None, LANES, (tc + 1) * LANES), lambda l, bi: (l, 0, 0)),
            pl.BlockSpec((None, tc, LANES, ns2), lambda l, bi: (l, 0, 0, 0)),
            pl.BlockSpec((None, ns2, tc * LANES), lambda l, bi: (l, 0, 0)),
            pl.BlockSpec((None, sp, ns2), lambda l, bi: (l, 0, 0)),
            pl.BlockSpec((None, sp, ns2), lambda l, bi: (l, 0, 0)),
            pl.BlockSpec((1, LANES), lambda l, bi: (0, l)),
        ],
        out_specs=pl.BlockSpec((None, seq, LANES), lambda l, bi: (bi, 0, l)),
        out_shape=jax.ShapeDtypeStruct((b, seq, ds), F32),
        scratch_shapes=[pltpu.VMEM((tc, nc, LANES), BF16)],
        compiler_params=_cparams(2),
        name=name,
    )(u, k_tab, f_tab, e_tab, a1, a2, d_skip.reshape(1, ds).astype(F32))


def _glu_kernel(y_ref, w_ref, b_ref, dst_ref, o_ref):
    del dst_ref
    yb = y_ref[...]
    z = jnp.dot(yb.astype(BF16), w_ref[...], preferred_element_type=F32) + b_ref[...]
    o_ref[...] = (yb * _sigmoid(z)).astype(o_ref.dtype)


def glu(y, w_glu, b_glu, dst, *, tm, name):
    m, d = y.shape
    assert m % tm == 0 and dst.shape[0] == m and dst.shape[1] % d == 0 and dst.dtype == BF16
    col_blk = dst.shape[1] // d - 1
    return pl.pallas_call(
        _glu_kernel,
        grid=(m // tm,),
        in_specs=[
            pl.BlockSpec((tm, d), lambda i: (i, 0)),
            pl.BlockSpec((d, d), lambda i: (0, 0)),
            pl.BlockSpec((1, d), lambda i: (0, 0)),
            pl.BlockSpec(memory_space=pl.ANY),
        ],
        out_specs=pl.BlockSpec((tm, d), lambda i: (i, col_blk)),
        out_shape=jax.ShapeDtypeStruct(dst.shape, BF16),
        input_output_aliases={3: 0},
        compiler_params=_cparams(1),
        name=name,
    )(y, w_glu, b_glu.reshape(1, d).astype(F32), dst)


def _softplus(x):
    return jnp.maximum(x, 0.0) + jnp.log(1.0 + jnp.exp(-jnp.abs(x)))


def _cumsum_rows(x, n):
    rows = lax.broadcasted_iota(jnp.int32, x.shape, 0)
    d = 1
    while d < n:
        x = x + jnp.where(rows >= d, pltpu.roll(x, d, 0), 0.0)
        d *= 2
    return x


def _cumsum_lane_segments(x, seg):
    pos = lax.broadcasted_iota(jnp.int32, x.shape, 1) & (seg - 1)
    d = 1
    while d < seg:
        x = x + jnp.where(pos >= d, pltpu.roll(x, d, 1), 0.0)
        d *= 2
    return x


DELTA_PACK = 4
DELTA_ROWS = DELTA_PACK * CHUNK
DELTA_LEVELS = CHUNK.bit_length() - 1


def _delta_level_map():
    import numpy as np
    i = np.arange(DELTA_ROWS)[:, None]
    j = np.arange(DELTA_ROWS)[None, :]
    x = i ^ j
    lev = np.zeros_like(x)
    for a in range(1, 8):
        lev += (x >> a) != 0
    out = np.where((i > j) & ((i // CHUNK) == (j // CHUNK)), lev, 99)
    out = np.where(i == j, -1, out)
    return out.astype(np.int32)


def _delta_kernel(q_ref, k_ref, v_ref, gate_ref, bgc_ref, gr_ref, cwq_ref, cwk_ref, cwv_ref,
                  pc_ref, pr_ref, nw_ref, lev_ref, o_ref, s_ref, halo_ref, *, hb, tb):
    c = CHUNK
    dk = DK_DELTA
    pk = DELTA_PACK
    rows = DELTA_ROWS
    sub = hb // pk
    scale = dk ** -0.5
    n_chunks = tb // c
    ti = pl.program_id(2)

    @pl.when(ti == 0)
    def _():
        s_ref[...] = jnp.zeros_like(s_ref)
        halo_ref[...] = jnp.zeros_like(halo_ref)

    alog_c = pc_ref[0:1, :]
    dtb_c = pc_ref[1:2, :]
    alog_r = pr_ref[0:sub, :]
    dtb_r = pr_ref[sub:2 * sub, :]
    nw = nw_ref[...]

    def conv_silu(win, cw_ref):
        acc = (cw_ref[3:4, :] * win[8:8 + c] + cw_ref[2:3, :] * win[7:7 + c]
               + cw_ref[1:2, :] * win[6:6 + c] + cw_ref[0:1, :] * win[5:5 + c])
        return acc * _sigmoid(acc)

    def stack_heads(x, s):
        return jnp.concatenate(
            [x[:, (pk * s + j) * dk:(pk * s + j + 1) * dk] for j in range(pk)], axis=0)

    def stack_cols(x, s, off):
        return jnp.concatenate(
            [x[:, off + pk * s + j:off + pk * s + j + 1] for j in range(pk)], axis=0)

    def chunk_body(ch, tails):
        tq, tk, tv = tails
        r0 = pl.multiple_of(ch * c, c)
        curq = q_ref[pl.ds(r0, c), :].astype(F32)
        curk = k_ref[pl.ds(r0, c), :].astype(F32)
        curv = v_ref[pl.ds(r0, c), :].astype(F32)
        qa = conv_silu(jnp.concatenate([tq, curq], axis=0), cwq_ref)
        ka = conv_silu(jnp.concatenate([tk, curk], axis=0), cwk_ref)
        va = conv_silu(jnp.concatenate([tv, curv], axis=0), cwv_ref)
        gate = gate_ref[pl.ds(r0, c), :].astype(F32)

        bg = bgc_ref[pl.ds(r0, c), :]
        beta_all = _sigmoid(bg)
        g_all = -jnp.exp(alog_c) * _softplus(bg + dtb_c)
        gc_all = _cumsum_rows(g_all, c)
        gl_all = jnp.broadcast_to(gc_all[c - 1:c, :], gc_all.shape)
        g_row = -jnp.exp(alog_r) * _softplus(gr_ref[ch] + dtb_r)
        gcr_all = _cumsum_lane_segments(g_row, c)

        lev = lev_ref[...]
        subs = []
        for s in range(sub):
            q = stack_heads(qa, s)
            k = stack_heads(ka, s)
            v = stack_heads(va, s)
            q = q * lax.rsqrt(jnp.sum(q * q, axis=-1, keepdims=True) + NORM_EPS)
            k = k * lax.rsqrt(jnp.sum(k * k, axis=-1, keepdims=True) + NORM_EPS)
            qs = q * scale
            beta = stack_cols(beta_all, s, 0)
            gc_col = stack_cols(gc_all, s, hb)
            gl_col = stack_cols(gl_all, s, hb)
            gc_row = gcr_all[s:s + 1, :]
            egc = jnp.exp(gc_col)
            kb = k * beta
            lhs = jnp.concatenate([kb, qs], axis=0).astype(BF16)
            kq = lax.dot_general(lhs, k.astype(BF16), (((1,), (1,)), ((), ())),
                                 preferred_element_type=F32)
            dec = jnp.exp(gc_col - gc_row)
            nm = kq[:rows] * dec
            subs.append(dict(
                nm=nm, qg=qs * egc,
                attn=jnp.where(lev < DELTA_LEVELS, kq[rows:] * dec, 0.0),
                x=jnp.concatenate([v * beta, kb * egc], axis=1),
                kd=(k * jnp.exp(gl_col - gc_col)).astype(BF16),
                t=jnp.where(lev == 0, -nm, jnp.where(lev == -1, 1.0, 0.0))))
        for a in range(1, DELTA_LEVELS):
            for st in subs:
                cm = jnp.where(lev == a, st["nm"], 0.0).astype(BF16)
                tb16 = st["t"].astype(BF16)
                pm = jnp.dot(tb16, cm, preferred_element_type=F32)
                st["t"] = st["t"] - jnp.dot(pm.astype(BF16), tb16, preferred_element_type=F32)
        for st in subs:
            t_off = jnp.where(lev == -1, 0.0, st["t"]).astype(BF16)
            x = st["x"]
            st["x"] = x + jnp.dot(t_off, x.astype(BF16), preferred_element_type=F32)
        for s, st in enumerate(subs):
            w_c = st["x"][:, dk:]
            qg = st["qg"]
            ws_parts, qs_parts = [], []
            for j in range(pk):
                h = pk * s + j
                rs = slice(j * c, (j + 1) * c)
                lhs2 = jnp.concatenate([w_c[rs], qg[rs]], axis=0).astype(BF16)
                ws = jnp.dot(lhs2, s_ref[h].astype(BF16), preferred_element_type=F32)
                ws_parts.append(ws[:c])
                qs_parts.append(ws[c:])
            st["v_new"] = st["x"][:, :dk] - jnp.concatenate(ws_parts, axis=0)
            st["qs"] = jnp.concatenate(qs_parts, axis=0)
        for st in subs:
            st["out"] = st["qs"] + jnp.dot(st["attn"].astype(BF16), st["v_new"].astype(BF16),
                                           preferred_element_type=F32)
        ri = lax.broadcasted_iota(jnp.int32, (rows, pk * dk), 0)
        cj = lax.broadcasted_iota(jnp.int32, (rows, pk * dk), 1)
        head_block = (ri // c) == (cj // dk)
        for st in subs:
            vbd = jnp.where(head_block, jnp.concatenate([st["v_new"]] * pk, axis=1), 0.0)
            st["upd"] = lax.dot_general(st["kd"], vbd.astype(BF16), (((0,), (0,)), ((), ())),
                                        preferred_element_type=F32)
        for s, st in enumerate(subs):
            out = st["out"]
            o = out * lax.rsqrt(jnp.mean(out * out, axis=-1, keepdims=True) + NORM_EPS) * nw
            gt = stack_heads(gate, s)
            res = (o * (gt * _sigmoid(gt))).astype(o_ref.dtype)
            for j in range(pk):
                h = pk * s + j
                e_last = jnp.exp(gc_all[c - 1:c, hb + h:hb + h + 1])
                s_ref[h] = s_ref[h] * e_last + st["upd"][:, j * dk:(j + 1) * dk]
                o_ref[pl.ds(r0, c), h * dk:(h + 1) * dk] = res[j * c:(j + 1) * c]

        return (curq[c - 8:], curk[c - 8:], curv[c - 8:])

    tails0 = (halo_ref[0], halo_ref[1], halo_ref[2])
    tq, tk, tv = lax.fori_loop(0, n_chunks, chunk_body, tails0)
    halo_ref[0] = tq
    halo_ref[1] = tk
    halo_ref[2] = tv


def gated_deltanet(proj, braw, conv_w, a_log, dt_bias, norm_w, n_heads, *, hb, tb, name):
    b, seq, _ = proj.shape
    dk = DK_DELTA
    c = CHUNK
    pk = DELTA_PACK
    assert n_heads % hb == 0 and hb % pk == 0 and seq % tb == 0 and tb % c == 0
    ng = n_heads // hb
    sub = hb // pk
    w = hb * dk
    nct = seq // c
    br = braw.reshape(b, seq, 2, ng, hb)
    bgc = jnp.transpose(br, (0, 3, 1, 2, 4)).reshape(b, ng, seq, 2 * hb)
    araw = braw[..., n_heads:].reshape(b, nct, c, ng, sub, pk)
    gr = jnp.transpose(araw, (0, 3, 1, 4, 5, 2)).reshape(b, ng, nct, sub, pk * c)
    al = a_log.astype(F32).reshape(ng, hb)
    dt = dt_bias.astype(F32).reshape(ng, hb)
    zeros = jnp.zeros((ng, hb), F32)
    pc = jnp.stack([jnp.concatenate([zeros, al], axis=1),
                    jnp.concatenate([zeros, dt], axis=1)], axis=1)
    rep = lambda p: jnp.repeat(p.reshape(ng, sub, pk), c, axis=-1)
    pr = jnp.concatenate([rep(al), rep(dt)], axis=1)
    cw = conv_w.astype(F32)
    lev = jnp.asarray(_delta_level_map())
    rows = pk * c
    kern = functools.partial(_delta_kernel, hb=hb, tb=tb)
    return pl.pallas_call(
        kern,
        grid=(b, ng, seq // tb),
        in_specs=[
            pl.BlockSpec((None, tb, w), lambda bi, g, t: (bi, t, g)),
            pl.BlockSpec((None, tb, w), lambda bi, g, t: (bi, t, ng + g)),
            pl.BlockSpec((None, tb, w), lambda bi, g, t: (bi, t, 2 * ng + g)),
            pl.BlockSpec((None, tb, w), lambda bi, g, t: (bi, t, 3 * ng + g)),
            pl.BlockSpec((None, None, tb, 2 * hb), lambda bi, g, t: (bi, g, t, 0)),
            pl.BlockSpec((None, None, tb // c, sub, rows), lambda bi, g, t: (bi, g, t, 0, 0)),
            pl.BlockSpec((CONV_K, w), lambda bi, g, t: (0, g)),
            pl.BlockSpec((CONV_K, w), lambda bi, g, t: (0, ng + g)),
            pl.BlockSpec((CONV_K, w), lambda bi, g, t: (0, 2 * ng + g)),
            pl.BlockSpec((None, 2, 2 * hb), lambda bi, g, t: (g, 0, 0)),
            pl.BlockSpec((None, 2 * sub, rows), lambda bi, g, t: (g, 0, 0)),
            pl.BlockSpec((1, dk), lambda bi, g, t: (0, 0)),
            pl.BlockSpec((rows, rows), lambda bi, g, t: (0, 0)),
        ],
        out_specs=pl.BlockSpec((None, tb, w), lambda bi, g, t: (bi, t, g)),
        out_shape=jax.ShapeDtypeStruct((b, seq, n_heads * dk), BF16),
        scratch_shapes=[pltpu.VMEM((hb, dk, dk), F32), pltpu.VMEM((3, 8, w), F32)],
        compiler_params=_cparams(3),
        name=name,
    )(proj, proj, proj, proj, bgc, gr, cw, cw, cw, pc, pr, norm_w.astype(F32).reshape(1, dk), lev)


FFN_UP_TN = 256


def _ffn(x32, x16, w_gate, w_up, w_down, gain, bias, layer):
    dff = w_gate.shape[2]
    dff_pad = -(-dff // FFN_UP_TN) * FFN_UP_TN
    h = swiglu_up(x16, w_gate, w_up, layer, n_out=dff_pad, tm=2048, tn=FFN_UP_TN, name=f"ffn_up_{layer}")
    wd = jnp.pad(w_down[layer].astype(BF16), ((0, dff_pad - dff), (0, 0)))[None]
    y = matmul(h, wd, 0, F32, col0=0, n=wd.shape[2], tm=1024, tn=256, resid=x32, alpha=ALPHA,
               name=f"ffn_down_{layer}")
    return layer_norm(y, gain, bias, tm=256, name=f"ffn_ln_{layer}")


def kernel(x, ab_w_in, ab_rel_bias, ab_lam_re, ab_lam_im, ab_log_step, ab_b_re, ab_b_im, ab_c_re, ab_c_im, ab_d_skip, ab_w_glu, ab_b_glu, ab_w_out, c_w_in, c_conv, c_a_log, c_dt_bias, c_norm_w, c_w_out, ffn_w_gate, ffn_w_up, ffn_w_down, ln_gain, ln_bias):
    b, seq, d = x.shape
    t = b * seq
    x32 = x.reshape(t, d).astype(F32)
    x16 = x32.astype(BF16)

    n_heads_a = ab_rel_bias.shape[1]
    d_attn = n_heads_a * DH_ATTN
    d_ssm = ab_w_in.shape[2] - 3 * d_attn
    qkv = matmul(x16, ab_w_in, 0, BF16, col0=0, n=3 * d_attn, tm=2048, tn=512, name="ab_in_qkv")
    u = matmul(x16, ab_w_in, 0, F32, col0=3 * d_attn, n=d_ssm, tm=2048, tn=512, name="ab_in_u")
    y_ab = attention(qkv.reshape(b, seq, 3 * d_attn), ab_rel_bias[0], n_heads_a, hb=2,
                     out_cols=d_attn + d_ssm, name="ab_attn")
    y_s = ssm_gelu(u.reshape(b, seq, d_ssm), ab_lam_re[0], ab_lam_im[0], ab_log_step[0], ab_b_re[0],
                   ab_b_im[0], ab_c_re[0], ab_c_im[0], ab_d_skip[0].reshape(-1), name="ab_ssm")
    y_ab = glu(y_s.reshape(t, d_ssm), ab_w_glu[0].astype(BF16), ab_b_glu[0],
               y_ab.reshape(t, d_attn + d_ssm), tm=512, name="ab_glu")
    y = matmul(y_ab, ab_w_out, 0, F32, col0=0, n=d, tm=1024, tn=512, resid=x32, alpha=ALPHA,
               name="ab_out")
    x32, x16 = layer_norm(y, ln_gain[0, 0], ln_bias[0, 0], tm=256, name="ab_ln")
    x32, x16 = _ffn(x32, x16, ffn_w_gate, ffn_w_up, ffn_w_down, ln_gain[0, 1], ln_bias[0, 1], 0)

    n_heads_c = c_a_log.shape[1]
    d_delta = n_heads_c * DK_DELTA
    cw_t = jnp.swapaxes(c_w_in, 1, 2)
    proj = matmul(x16, cw_t, 0, BF16, col0=0, n=4 * d_delta, tm=2048, tn=512, transposed=True,
                  name="c_in")
    braw = matmul(x16, cw_t[:, 4 * d_delta:], 0, F32, col0=0, n=2 * n_heads_c, tm=2048,
                  tn=2 * n_heads_c, transposed=True, name="c_in_small")
    o = gated_deltanet(proj.reshape(b, seq, 4 * d_delta), braw.reshape(b, seq, 2 * n_heads_c),
                       c_conv[0], c_a_log[0], c_dt_bias[0], c_norm_w[0], n_heads_c,
                       hb=16, tb=512, name="c_delta")
    y = matmul(o.reshape(t, d_delta), c_w_out, 0, F32, col0=0, n=d, tm=1024, tn=512, resid=x32,
               alpha=ALPHA, name="c_out")
    x32, x16 = layer_norm(y, ln_gain[1, 0], ln_bias[1, 0], tm=256, name="c_ln")
    x32, x16 = _ffn(x32, x16, ffn_w_gate, ffn_w_up, ffn_w_down, ln_gain[1, 1], ln_bias[1, 1], 1)
    return x32.reshape(b, seq, d).astype(x.dtype)
```

```python
import functools
import math

import jax
import jax.numpy as jnp
from jax import lax
from jax.experimental import pallas as pl
from jax.experimental.pallas import tpu as pltpu

F32 = jnp.float32
BF16 = jnp.bfloat16

CHUNK = 64
LEFT_CHUNKS = 8
DH_ATTN = 128
MAX_REL = 256
SSM_GROUP = 16
SSM_STATE = 64
DK_DELTA = 128
CONV_K = 4
DEPTH = 2
ALPHA = (2.0 * DEPTH) ** 0.25
LN_EPS = 1e-5
NORM_EPS = 1e-6
NEG_BIG = -1e30

V7X_VMEM_LIMIT_BYTES = 56 * 1024 * 1024
LANES = 128
SSM_BLOCK = 16
HI = lax.Precision.HIGHEST


def _cparams(n_axes, flags=None):
    return pltpu.CompilerParams(
        dimension_semantics=("arbitrary",) * n_axes,
        vmem_limit_bytes=V7X_VMEM_LIMIT_BYTES,
        flags=flags,
    )


def _sigmoid(x):
    return 0.5 * jnp.tanh(0.5 * x) + 0.5


def _mm_kernel(x_ref, w_ref, *rest, alpha, transposed):
    o_ref = rest[-1]
    w = w_ref[...].astype(BF16)
    dims = (((1,), (1,)), ((), ())) if transposed else (((1,), (0,)), ((), ()))
    y = lax.dot_general(x_ref[...], w, dims, preferred_element_type=F32)
    if len(rest) == 2:
        y = alpha * rest[0][...] + y
    o_ref[...] = y.astype(o_ref.dtype)


def matmul(x, w, layer, out_dtype, *, col0, n, tm, tn, name, transposed=False, resid=None, alpha=None):
    m, k = x.shape
    assert m % tm == 0 and n % tn == 0 and col0 % tn == 0
    j0 = col0 // tn
    if transposed:
        w_spec = pl.BlockSpec((None, tn, k), lambda i, j: (layer, j0 + j, 0))
    else:
        w_spec = pl.BlockSpec((None, k, tn), lambda i, j: (layer, 0, j0 + j))
    in_specs = [pl.BlockSpec((tm, k), lambda i, j: (i, 0), pipeline_mode=pl.Buffered(1)), w_spec]
    args = [x, w]
    if resid is not None:
        in_specs.append(pl.BlockSpec((tm, tn), lambda i, j: (i, j)))
        args.append(resid)
    return pl.pallas_call(
        functools.partial(_mm_kernel, alpha=alpha, transposed=transposed),
        grid=(m // tm, n // tn),
        in_specs=in_specs,
        out_specs=pl.BlockSpec((tm, tn), lambda i, j: (i, j)),
        out_shape=jax.ShapeDtypeStruct((m, n), out_dtype),
        compiler_params=_cparams(2),
        name=name,
    )(*args)


def _swiglu_kernel(x_ref, wg_ref, wu_ref, o_ref, *, n_valid_blocks):
    x = x_ref[...]
    g = jnp.dot(x, wg_ref[...].astype(BF16), preferred_element_type=F32)
    sg = g * _sigmoid(g)
    u = jnp.dot(x, wu_ref[...].astype(BF16), preferred_element_type=F32)
    h = sg * u
    h = jnp.where(pl.program_id(1) < n_valid_blocks, h, 0.0)
    o_ref[...] = h.astype(o_ref.dtype)


def swiglu_up(x, wg, wu, layer, *, n_out, tm, tn, name):
    m, k = x.shape
    n = wg.shape[2]
    assert m % tm == 0 and n % tn == 0 and n_out % tn == 0 and n_out >= n
    nvb = n // tn
    wmap = lambda i, j: (layer, 0, jnp.minimum(j, nvb - 1))
    return pl.pallas_call(
        functools.partial(_swiglu_kernel, n_valid_blocks=nvb),
        grid=(m // tm, n_out // tn),
        in_specs=[
            pl.BlockSpec((tm, k), lambda i, j: (i, 0), pipeline_mode=pl.Buffered(1)),
            pl.BlockSpec((None, k, tn), wmap),
            pl.BlockSpec((None, k, tn), wmap),
        ],
        out_specs=pl.BlockSpec((tm, tn), lambda i, j: (i, j)),
        out_shape=jax.ShapeDtypeStruct((m, n_out), BF16),
        compiler_params=_cparams(2),
        name=name,
    )(x, wg, wu)


def _ln_kernel(y_ref, g_ref, b_ref, o32_ref, o16_ref):
    y = y_ref[...]
    mu = jnp.mean(y, axis=-1, keepdims=True)
    yc = y - mu
    var = jnp.mean(yc * yc, axis=-1, keepdims=True)
    x = yc * lax.rsqrt(var + LN_EPS) * g_ref[...] + b_ref[...]
    o32_ref[...] = x
    o16_ref[...] = x.astype(BF16)


def layer_norm(y, gain, bias, *, tm, name):
    m, d = y.shape
    assert m % tm == 0
    return pl.pallas_call(
        _ln_kernel,
        grid=(m // tm,),
        in_specs=[
            pl.BlockSpec((tm, d), lambda i: (i, 0)),
            pl.BlockSpec((1, d), lambda i: (0, 0)),
            pl.BlockSpec((1, d), lambda i: (0, 0)),
        ],
        out_specs=[
            pl.BlockSpec((tm, d), lambda i: (i, 0)),
            pl.BlockSpec((tm, d), lambda i: (i, 0)),
        ],
        out_shape=[jax.ShapeDtypeStruct((m, d), F32), jax.ShapeDtypeStruct((m, d), BF16)],
        compiler_params=_cparams(1),
        name=name,
    )(y, gain.reshape(1, d), bias.reshape(1, d))


ATTN_QB = 4 * CHUNK
ATTN_WIN = ATTN_QB + LEFT_CHUNKS * CHUNK
ATTN_NVAR = LEFT_CHUNKS * CHUNK // ATTN_QB + 1


def _attn_bias_tiles(rel_bias):
    n_heads = rel_bias.shape[0]
    period = ATTN_QB + ATTN_WIN
    m = jnp.arange(period)
    tiles = []
    for v in range(ATTN_NVAR):
        rel = v * ATTN_QB - jnp.where(m < ATTN_WIN, m, m - period)
        row = rel_bias.astype(F32)[:, jnp.clip(rel, -MAX_REL, MAX_REL) + MAX_REL]
        flat = jnp.tile(row, (1, ATTN_QB))[:, :ATTN_QB * (period - 1)]
        skew = flat.reshape(n_heads, ATTN_QB, period - 1)[:, :, :ATTN_WIN]
        tiles.append(skew)
    tiles = jnp.stack(tiles)
    i = jnp.arange(ATTN_QB)[None, :, None]
    j = jnp.arange(ATTN_WIN)[None, None, :]
    qpos = (jnp.arange(ATTN_NVAR) * ATTN_QB)[:, None, None] + i
    dchunk = qpos // CHUNK - j // CHUNK
    valid = (dchunk >= 0) & (dchunk <= LEFT_CHUNKS)
    return jnp.where(valid[:, None], tiles, NEG_BIG)


def _attn_kernel(q_ref, k_ref, v_ref, bias_ref, o_ref, *, hb, seq):
    scale = DH_ATTN ** -0.5
    nq = seq // ATTN_QB

    def body(qi, carry):
        qs = pl.multiple_of(qi * ATTN_QB, ATTN_QB)
        start = pl.multiple_of(jnp.maximum(qs - LEFT_CHUNKS * CHUNK, 0), ATTN_QB)
        var = jnp.minimum(qi, ATTN_NVAR - 1)
        heads = [slice(h * DH_ATTN, (h + 1) * DH_ATTN) for h in range(hb)]
        s_all = [lax.dot_general(q_ref[pl.ds(qs, ATTN_QB), c], k_ref[pl.ds(start, ATTN_WIN), c],
                                 (((1,), (1,)), ((), ())), preferred_element_type=F32)
                 for c in heads]
        p_all = []
        for h, s in enumerate(s_all):
            s = s * scale + bias_ref[var, h]
            m = jnp.max(s, axis=-1, keepdims=True)
            p = jnp.exp(s - m)
            p_all.append((p.astype(BF16), 1.0 / jnp.sum(p, axis=-1, keepdims=True)))
        for c, (p, inv_l) in zip(heads, p_all):
            o = jnp.dot(p, v_ref[pl.ds(start, ATTN_WIN), c], preferred_element_type=F32) * inv_l
            o_ref[pl.ds(qs, ATTN_QB), c] = o.astype(o_ref.dtype)
        return carry

    lax.fori_loop(0, nq, body, 0)


def attention(qkv, rel_bias, n_heads, *, hb, out_cols, name):
    b, seq, _ = qkv.shape
    assert n_heads % hb == 0 and seq % ATTN_QB == 0 and seq >= ATTN_WIN
    ng = n_heads // hb
    wcols = hb * DH_ATTN
    bias = _attn_bias_tiles(rel_bias)
    kern = functools.partial(_attn_kernel, hb=hb, seq=seq)
    return pl.pallas_call(
        kern,
        grid=(b, ng),
        in_specs=[
            pl.BlockSpec((None, seq, wcols), lambda bi, g: (bi, 0, g)),
            pl.BlockSpec((None, seq, wcols), lambda bi, g: (bi, 0, ng + g)),
            pl.BlockSpec((None, seq, wcols), lambda bi, g: (bi, 0, 2 * ng + g)),
            pl.BlockSpec((ATTN_NVAR, hb, ATTN_QB, ATTN_WIN), lambda bi, g: (0, g, 0, 0)),
        ],
        out_specs=pl.BlockSpec((None, seq, wcols), lambda bi, g: (bi, 0, g)),
        out_shape=jax.ShapeDtypeStruct((b, seq, out_cols), BF16),
        compiler_params=_cparams(2),
        name=name,
    )(qkv, qkv, qkv, bias)


SSM_TILE_GROUPS = LANES // SSM_GROUP
SSM_TILE_STATES = SSM_TILE_GROUPS * SSM_STATE


def _ssm_tables(lam_re, lam_im, log_step, b_re, b_im, c_re, c_im, n_scan_steps):
    g, p = lam_re.shape
    hg = b_re.shape[-1]
    tc = SSM_BLOCK
    tg = SSM_TILE_GROUPS
    nt = g // tg
    lam = lax.complex(jnp.minimum(lam_re.astype(F32), -1e-4), lam_im.astype(F32))
    step = jnp.exp(log_step.astype(F32))[:, None]
    lam_bar = jnp.exp(lam * step)
    b_bar = ((lam_bar - 1.0) / lam)[..., None] * lax.complex(b_re.astype(F32), b_im.astype(F32))
    pw = [jnp.ones_like(lam_bar)]
    for _ in range(tc):
        pw.append(pw[-1] * lam_bar)
    pw = jnp.stack(pw)
    xb = pw[:tc, :, :, None] * b_bar[None]
    xb_re, xb_im = jnp.real(xb), jnp.imag(xb)
    cr, ci = c_re.astype(F32), c_im.astype(F32)
    kmat = (jnp.einsum('ghp,tgpi->tghi', cr, xb_re, precision=HI)
            - jnp.einsum('ghp,tgpi->tghi', ci, xb_im, precision=HI))
    def diag_mask(rows_per_group, cols_per_group, col_groups_stride):
        r = jnp.arange(tg * rows_per_group)[:, None] // rows_per_group
        c = (jnp.arange(col_groups_stride)[None, :] // cols_per_group) % tg
        return r == c
    kc = jnp.transpose(kmat.reshape(tc, nt, tg, hg, hg), (1, 2, 4, 0, 3))
    kc = jnp.tile(kc.reshape(nt, tg * hg, tc, hg), (1, 1, 1, tg)).reshape(nt, tg * hg, tc * tg * hg)
    k_tab = jnp.where(diag_mask(hg, hg, tc * tg * hg), kc, 0.0).astype(BF16)
    k_tab = jnp.pad(k_tab, ((0, 0), (0, 0), (tg * hg, 0)))
    def f_part(xpart):
        xr = jnp.transpose(xpart[::-1].reshape(tc, nt, tg, p, hg), (1, 0, 2, 4, 3))
        xr = jnp.tile(xr.reshape(nt, tc, tg * hg, p), (1, 1, 1, tg))
        return jnp.where(diag_mask(hg, p, tg * p), xr, 0.0).astype(BF16)
    f_tab = jnp.concatenate([f_part(xb_re), f_part(xb_im)], axis=-1)
    gmat = lax.complex(cr, ci)[None] * pw[1:, :, None, :]
    def e_part(gpart):
        gr = jnp.transpose(gpart.reshape(tc, nt, tg, hg, p), (1, 2, 4, 0, 3))
        gr = jnp.tile(gr.reshape(nt, tg * p, tc, hg), (1, 1, 1, tg)).reshape(nt, tg * p, tc * tg * hg)
        return jnp.where(diag_mask(p, hg, tc * tg * hg), gr, 0.0).astype(BF16)
    e_tab = jnp.concatenate([e_part(jnp.real(gmat)), e_part(-jnp.imag(gmat))], axis=1)
    a = pw[tc]
    a1, a2 = [], []
    for _ in range(n_scan_steps):
        ar = jnp.real(a).reshape(nt, tg * p)
        ai = jnp.imag(a).reshape(nt, tg * p)
        a1.append(jnp.concatenate([ar, ar], axis=-1))
        a2.append(jnp.concatenate([-ai, ai], axis=-1))
        a = a * a
    pad = (-n_scan_steps) % 8
    a1 = jnp.pad(jnp.stack(a1, axis=1), ((0, 0), (0, pad), (0, 0)))
    a2 = jnp.pad(jnp.stack(a2, axis=1), ((0, 0), (0, pad), (0, 0)))
    return k_tab, f_tab, e_tab, a1, a2


def _gelu_tanh(x):
    c = math.sqrt(2.0 / math.pi)
    return 0.5 * x * (1.0 + jnp.tanh(c * (x + 0.044715 * (x * x * x))))


def _ssm_kernel(u_ref, k_ref, f_ref, e_ref, a1_ref, a2_ref, d_ref, y_ref, xs_ref, *, nc, nsteps):
    tc = SSM_BLOCK
    ns = SSM_TILE_STATES

    def frames(s):
        return u_ref[pl.ds(s, nc, stride=tc), :]

    sloc = None
    for s in range(tc):
        xs = frames(s).astype(BF16)
        xs_ref[s] = xs
        part = jnp.dot(xs, f_ref[s], preferred_element_type=F32)
        sloc = part if sloc is None else sloc + part
    x = sloc
    for k in range(nsteps):
        d = 1 << k
        sh = jnp.concatenate([jnp.zeros((d, 2 * ns), F32), x[:nc - d]], axis=0)
        sw = pltpu.roll(sh, ns, 1)
        x = x + a1_ref[k:k + 1, :] * sh + a2_ref[k:k + 1, :] * sw
    sprev = jnp.concatenate([jnp.zeros((1, 2 * ns), F32), x[:nc - 1]], axis=0).astype(BF16)
    dskip = d_ref[...]
    for t in range(0, tc, 2):
        acc = jnp.dot(sprev, e_ref[:, t * LANES:(t + 2) * LANES], preferred_element_type=F32)
        for s in range(t + 2):
            c0 = (t - s + 1) * LANES
            acc = acc + jnp.dot(xs_ref[s], k_ref[:, c0:c0 + 2 * LANES], preferred_element_type=F32)
        for j in range(2):
            y = acc[:, j * LANES:(j + 1) * LANES] + dskip * frames(t + j)
            y_ref[pl.ds(t + j, nc, stride=tc), :] = _gelu_tanh(y)


def ssm_gelu(u, lam_re, lam_im, log_step, b_re, b_im, c_re, c_im, d_skip, *, name):
    b, seq, ds = u.shape
    g = lam_re.shape[0]
    tc = SSM_BLOCK
    nc = seq // tc
    assert seq % tc == 0 and ds == g * SSM_GROUP and ds % LANES == 0 and nc & (nc - 1) == 0
    assert lam_re.shape[1] == SSM_STATE
    nt = ds // LANES
    nsteps = nc.bit_length() - 1
    k_tab, f_tab, e_tab, a1, a2 = _ssm_tables(lam_re, lam_im, log_step, b_re, b_im, c_re, c_im, nsteps)
    ns2 = 2 * SSM_TILE_STATES
    sp = a1.shape[1]
    kern = functools.partial(_ssm_kernel, nc=nc, nsteps=nsteps)
    return pl.pallas_call(
        kern,
        grid=(nt, b),
        in_specs=[
            pl.BlockSpec((None, seq, LANES), lambda l, bi: (bi, 0, l)),
            pl.BlockSpec((None, LANES, (tc + 1) * LANES), lambda l, bi: (l, 0, 0)),
            pl.BlockSpec((None, tc, LANES, ns2), lambda l, bi: (l, 0, 0, 0)),
            pl.BlockSpec((None, ns2, tc * LANES), lambda l, bi: (l, 0, 0)),
            pl.BlockSpec((None, sp, ns2), lambda l, bi: (l, 0, 0)),
            pl.BlockSpec((None, sp, ns2), lambda l, bi: (l, 0, 0)),
            pl.BlockSpec((1, LANES), lambda l, bi: (0, l)),
        ],
        out_specs=pl.BlockSpec((None, seq, LANES), lambda l, bi: (bi, 0, l)),
        out_shape=jax.ShapeDtypeStruct((b, seq, ds), F32),
        scratch_shapes=[pltpu.VMEM((tc, nc, LANES), BF16)],
        compiler_params=_cparams(2),
        name=name,
    )(u, k_tab, f_tab, e_tab, a1, a2, d_skip.reshape(1, ds).astype(F32))


def _glu_kernel(y_ref, w_ref, b_ref, dst_ref, o_ref):
    del dst_ref
    yb = y_ref[...]
    z = jnp.dot(yb.astype(BF16), w_ref[...], preferred_element_type=F32) + b_ref[...]
    o_ref[...] = (yb * _sigmoid(z)).astype(o_ref.dtype)


def glu(y, w_glu, b_glu, dst, *, tm, name):
    m, d = y.shape
    assert m % tm == 0 and dst.shape[0] == m and dst.shape[1] % d == 0 and dst.dtype == BF16
    col_blk = dst.shape[1] // d - 1
    return pl.pallas_call(
        _glu_kernel,
        grid=(m // tm,),
        in_specs=[
            pl.BlockSpec((tm, d), lambda i: (i, 0)),
            pl.BlockSpec((d, d), lambda i: (0, 0)),
            pl.BlockSpec((1, d), lambda i: (0, 0)),
            pl.BlockSpec(memory_space=pl.ANY),
        ],
        out_specs=pl.BlockSpec((tm, d), lambda i: (i, col_blk)),
        out_shape=jax.ShapeDtypeStruct(dst.shape, BF16),
        input_output_aliases={3: 0},
        compiler_params=_cparams(1),
        name=name,
    )(y, w_glu, b_glu.reshape(1, d).astype(F32), dst)


def _softplus(x):
    return jnp.maximum(x, 0.0) + jnp.log(1.0 + jnp.exp(-jnp.abs(x)))


def _cumsum_rows(x, n):
    rows = lax.broadcasted_iota(jnp.int32, x.shape, 0)
    d = 1
    while d < n:
        x = x + jnp.where(rows >= d, pltpu.roll(x, d, 0), 0.0)
        d *= 2
    return x


def _cumsum_lane_segments(x, seg):
    pos = lax.broadcasted_iota(jnp.int32, x.shape, 1) & (seg - 1)
    d = 1
    while d < seg:
        x = x + jnp.where(pos >= d, pltpu.roll(x, d, 1), 0.0)
        d *= 2
    return x


DELTA_PACK = 4
DELTA_ROWS = DELTA_PACK * CHUNK
DELTA_LEVELS = CHUNK.bit_length() - 1


def _delta_level_map():
    import numpy as np
    i = np.arange(DELTA_ROWS)[:, None]
    j = np.arange(DELTA_ROWS)[None, :]
    x = i ^ j
    lev = np.zeros_like(x)
    for a in range(1, 8):
        lev += (x >> a) != 0
    out = np.where((i > j) & ((i // CHUNK) == (j // CHUNK)), lev, 99)
    out = np.where(i == j, -1, out)
    return out.astype(np.int32)


def _delta_kernel(q_ref, k_ref, v_ref, gate_ref, bgc_ref, gr_ref, cwq_ref, cwk_ref, cwv_ref,
                  pc_ref, pr_ref, nw_ref, lev_ref, o_ref, s_ref, halo_ref, *, hb, tb):
    c = CHUNK
    dk = DK_DELTA
    pk = DELTA_PACK
    rows = DELTA_ROWS
    sub = hb // pk
    sw = pk * dk
    scale = dk ** -0.5
    n_chunks = tb // c
    ti = pl.program_id(2)

    @pl.when(ti == 0)
    def _():
        s_ref[...] = jnp.zeros_like(s_ref)
        halo_ref[...] = jnp.zeros_like(halo_ref)

    alog_c = pc_ref[0:1, :]
    dtb_c = pc_ref[1:2, :]
    alog_r = pr_ref[0:sub, :]
    dtb_r = pr_ref[sub:2 * sub, :]
    nw = nw_ref[...]

    def conv_silu(tail, cur, cw_ref, lanes):
        win = jnp.concatenate([tail, cur], axis=0)
        acc = (cw_ref[3:4, lanes] * win[8:8 + c] + cw_ref[2:3, lanes] * win[7:7 + c]
               + cw_ref[1:2, lanes] * win[6:6 + c] + cw_ref[0:1, lanes] * win[5:5 + c])
        return acc * _sigmoid(acc)

    def stack_heads(x):
        return jnp.concatenate([x[:, j * dk:(j + 1) * dk] for j in range(pk)], axis=0)

    def stack_cols(x, s, off):
        return jnp.concatenate(
            [x[:, off + pk * s + j:off + pk * s + j + 1] for j in range(pk)], axis=0)

    def chunk_stages(ch, get_tails):
        r0 = pl.multiple_of(ch * c, c)
        bg = bgc_ref[pl.ds(r0, c), :]
        beta_all = _sigmoid(bg)
        g_all = -jnp.exp(alog_c) * _softplus(bg + dtb_c)
        gc_all = _cumsum_rows(g_all, c)
        gl_all = jnp.broadcast_to(gc_all[c - 1:c, :], gc_all.shape)
        g_row = -jnp.exp(alog_r) * _softplus(gr_ref[ch] + dtb_r)
        gcr_all = _cumsum_lane_segments(g_row, c)
        lev = lev_ref[...]
        ri = lax.broadcasted_iota(jnp.int32, (rows, pk * dk), 0)
        cj = lax.broadcasted_iota(jnp.int32, (rows, pk * dk), 1)
        head_block = (ri // c) == (cj // dk)
        subs = [dict() for _ in range(sub)]

        def prep(s, st):
            tq, tk, tv = get_tails()
            lanes = slice(s * sw, (s + 1) * sw)
            curq = q_ref[pl.ds(r0, c), lanes].astype(F32)
            curk = k_ref[pl.ds(r0, c), lanes].astype(F32)
            curv = v_ref[pl.ds(r0, c), lanes].astype(F32)
            st["tails"] = (curq[c - 8:], curk[c - 8:], curv[c - 8:])
            q = stack_heads(conv_silu(tq[:, lanes], curq, cwq_ref, lanes))
            k = stack_heads(conv_silu(tk[:, lanes], curk, cwk_ref, lanes))
            v = stack_heads(conv_silu(tv[:, lanes], curv, cwv_ref, lanes))
            q = q * lax.rsqrt(jnp.sum(q * q, axis=-1, keepdims=True) + NORM_EPS)
            k = k * lax.rsqrt(jnp.sum(k * k, axis=-1, keepdims=True) + NORM_EPS)
            qs = q * scale
            beta = stack_cols(beta_all, s, 0)
            gc_col = stack_cols(gc_all, s, hb)
            gl_col = stack_cols(gl_all, s, hb)
            gc_row = gcr_all[s:s + 1, :]
            egc = jnp.exp(gc_col)
            kb = k * beta
            lhs = jnp.concatenate([kb, qs], axis=0).astype(BF16)
            kq = lax.dot_general(lhs, k.astype(BF16), (((1,), (1,)), ((), ())),
                                 preferred_element_type=F32)
            dec = jnp.exp(gc_col - gc_row)
            nm = kq[:rows] * dec
            st.update(
                nm=nm, qg=qs * egc,
                attn=jnp.where(lev < DELTA_LEVELS, kq[rows:] * dec, 0.0),
                x=jnp.concatenate([v * beta, kb * egc], axis=1),
                kd=(k * jnp.exp(gl_col - gc_col)).astype(BF16),
                t=jnp.where(lev == 0, -nm, jnp.where(lev == -1, 1.0, 0.0)))

        def level(a):
            def run(s, st):
                cm = jnp.where(lev == a, st["nm"], 0.0).astype(BF16)
                tb16 = st["t"].astype(BF16)
                pm = jnp.dot(tb16, cm, preferred_element_type=F32)
                st["t"] = st["t"] - jnp.dot(pm.astype(BF16), tb16, preferred_element_type=F32)
            return run

        def solve(s, st):
            t_off = jnp.where(lev == -1, 0.0, st["t"]).astype(BF16)
            x = st["x"]
            st["x"] = x + jnp.dot(t_off, x.astype(BF16), preferred_element_type=F32)

        def read_state(s, st):
            w_c = st["x"][:, dk:]
            qg = st["qg"]
            ws_parts, qs_parts = [], []
            for j in range(pk):
                h = pk * s + j
                rs = slice(j * c, (j + 1) * c)
                lhs2 = jnp.concatenate([w_c[rs], qg[rs]], axis=0).astype(BF16)
                ws = jnp.dot(lhs2, s_ref[h].astype(BF16), preferred_element_type=F32)
                ws_parts.append(ws[:c])
                qs_parts.append(ws[c:])
            st["v_new"] = st["x"][:, :dk] - jnp.concatenate(ws_parts, axis=0)
            st["qs"] = jnp.concatenate(qs_parts, axis=0)

        def outputs(s, st):
            st["out"] = st["qs"] + jnp.dot(st["attn"].astype(BF16), st["v_new"].astype(BF16),
                                           preferred_element_type=F32)
            vbd = jnp.where(head_block, jnp.concatenate([st["v_new"]] * pk, axis=1), 0.0)
            st["upd"] = lax.dot_general(st["kd"], vbd.astype(BF16), (((0,), (0,)), ((), ())),
                                        preferred_element_type=F32)

        def finish(s, st):
            out = st["out"]
            o = out * lax.rsqrt(jnp.mean(out * out, axis=-1, keepdims=True) + NORM_EPS) * nw
            gt = stack_heads(gate_ref[pl.ds(r0, c), s * sw:(s + 1) * sw].astype(F32))
            res = (o * (gt * _sigmoid(gt))).astype(o_ref.dtype)
            for j in range(pk):
                h = pk * s + j
                e_last = jnp.exp(gc_all[c - 1:c, hb + h:hb + h + 1])
                s_ref[h] = s_ref[h] * e_last + st["upd"][:, j * dk:(j + 1) * dk]
                o_ref[pl.ds(r0, c), h * dk:(h + 1) * dk] = res[j * c:(j + 1) * c]

        stages = [prep] + [level(a) for a in range(1, DELTA_LEVELS)] + [solve, read_state, outputs,
                                                                        finish]
        bound = [functools.partial(lambda f, s: f(s, subs[s]), f) for f in stages]
        tails_out = lambda: tuple(
            jnp.concatenate([st["tails"][i] for st in subs], axis=1) for i in range(3))
        return bound, tails_out

    def pair_body(it, tails):
        st_a, tails_a = chunk_stages(2 * it, lambda: tails)
        st_b, tails_b = chunk_stages(2 * it + 1, tails_a)
        n_lev = DELTA_LEVELS - 1
        prep_a, lev_a, rest_a = st_a[0], st_a[1:1 + n_lev], st_a[1 + n_lev:]
        prep_b, lev_b, rest_b = st_b[0], st_b[1:1 + n_lev], st_b[1 + n_lev:]
        for s in range(sub):
            prep_a(s)
        for a in range(n_lev):
            for s in range(sub):
                lev_a[a](s)
            for s in range(a * sub // n_lev, (a + 1) * sub // n_lev):
                prep_b(s)
        for s in range(sub):
            rest_a[0](s)
        for k, f_a in enumerate(rest_a[1:]):
            for s in range(sub):
                f_a(s)
                lev_b[k](s)
        for f_b in lev_b[len(rest_a) - 1:] + rest_b:
            for s in range(sub):
                f_b(s)
        return tails_b()

    assert n_chunks % 2 == 0
    tails0 = (halo_ref[0], halo_ref[1], halo_ref[2])
    tq, tk, tv = lax.fori_loop(0, n_chunks // 2, pair_body, tails0)
    halo_ref[0] = tq
    halo_ref[1] = tk
    halo_ref[2] = tv


def gated_deltanet(proj, braw, conv_w, a_log, dt_bias, norm_w, n_heads, *, hb, tb, name):
    b, seq, _ = proj.shape
    dk = DK_DELTA
    c = CHUNK
    pk = DELTA_PACK
    assert n_heads % hb == 0 and hb % pk == 0 and seq % tb == 0 and tb % c == 0
    ng = n_heads // hb
    sub = hb // pk
    w = hb * dk
    nct = seq // c
    br = braw.reshape(b, seq, 2, ng, hb)
    bgc = jnp.transpose(br, (0, 3, 1, 2, 4)).reshape(b, ng, seq, 2 * hb)
    araw = braw[..., n_heads:].reshape(b, nct, c, ng, sub, pk)
    gr = jnp.transpose(araw, (0, 3, 1, 4, 5, 2)).reshape(b, ng, nct, sub, pk * c)
    al = a_log.astype(F32).reshape(ng, hb)
    dt = dt_bias.astype(F32).reshape(ng, hb)
    zeros = jnp.zeros((ng, hb), F32)
    pc = jnp.stack([jnp.concatenate([zeros, al], axis=1),
                    jnp.concatenate([zeros, dt], axis=1)], axis=1)
    rep = lambda p: jnp.repeat(p.reshape(ng, sub, pk), c, axis=-1)
    pr = jnp.concatenate([rep(al), rep(dt)], axis=1)
    cw = conv_w.astype(F32)
    lev = jnp.asarray(_delta_level_map())
    rows = pk * c
    kern = functools.partial(_delta_kernel, hb=hb, tb=tb)
    return pl.pallas_call(
        kern,
        grid=(b, ng, seq // tb),
        in_specs=[
            pl.BlockSpec((None, tb, w), lambda bi, g, t: (bi, t, g)),
            pl.BlockSpec((None, tb, w), lambda bi, g, t: (bi, t, ng + g)),
            pl.BlockSpec((None, tb, w), lambda bi, g, t: (bi, t, 2 * ng + g)),
            pl.BlockSpec((None, tb, w), lambda bi, g, t: (bi, t, 3 * ng + g)),
            pl.BlockSpec((None, None, tb, 2 * hb), lambda bi, g, t: (bi, g, t, 0)),
            pl.BlockSpec((None, None, tb // c, sub, rows), lambda bi, g, t: (bi, g, t, 0, 0)),
            pl.BlockSpec((CONV_K, w), lambda bi, g, t: (0, g)),
            pl.BlockSpec((CONV_K, w), lambda bi, g, t: (0, ng + g)),
            pl.BlockSpec((CONV_K, w), lambda bi, g, t: (0, 2 * ng + g)),
            pl.BlockSpec((None, 2, 2 * hb), lambda bi, g, t: (g, 0, 0)),
            pl.BlockSpec((None, 2 * sub, rows), lambda bi, g, t: (g, 0, 0)),
            pl.BlockSpec((1, dk), lambda bi, g, t: (0, 0)),
            pl.BlockSpec((rows, rows), lambda bi, g, t: (0, 0)),
        ],
        out_specs=pl.BlockSpec((None, tb, w), lambda bi, g, t: (bi, t, g)),
        out_shape=jax.ShapeDtypeStruct((b, seq, n_heads * dk), BF16),
        scratch_shapes=[pltpu.VMEM((hb, dk, dk), F32), pltpu.VMEM((3, 8, w), F32)],
        compiler_params=_cparams(3),
        name=name,
    )(proj, proj, proj, proj, bgc, gr, cw, cw, cw, pc, pr, norm_w.astype(F32).reshape(1, dk), lev)


FFN_UP_TN = 256


def _ffn(x32, x16, w_gate, w_up, w_down, gain, bias, layer):
    dff = w_gate.shape[2]
    dff_pad = -(-dff // FFN_UP_TN) * FFN_UP_TN
    h = swiglu_up(x16, w_gate, w_up, layer, n_out=dff_pad, tm=2048, tn=FFN_UP_TN, name=f"ffn_up_{layer}")
    wd = jnp.pad(w_down[layer].astype(BF16), ((0, dff_pad - dff), (0, 0)))[None]
    y = matmul(h, wd, 0, F32, col0=0, n=wd.shape[2], tm=1024, tn=256, resid=x32, alpha=ALPHA,
               name=f"ffn_down_{layer}")
    return layer_norm(y, gain, bias, tm=256, name=f"ffn_ln_{layer}")


def kernel(x, ab_w_in, ab_rel_bias, ab_lam_re, ab_lam_im, ab_log_step, ab_b_re, ab_b_im, ab_c_re, ab_c_im, ab_d_skip, ab_w_glu, ab_b_glu, ab_w_out, c_w_in, c_conv, c_a_log, c_dt_bias, c_norm_w, c_w_out, ffn_w_gate, ffn_w_up, ffn_w_down, ln_gain, ln_bias):
    b, seq, d = x.shape
    t = b * seq
    x32 = x.reshape(t, d).astype(F32)
    x16 = x32.astype(BF16)

    n_heads_a = ab_rel_bias.shape[1]
    d_attn = n_heads_a * DH_ATTN
    d_ssm = ab_w_in.shape[2] - 3 * d_attn
    qkv = matmul(x16, ab_w_in, 0, BF16, col0=0, n=3 * d_attn, tm=2048, tn=512, name="ab_in_qkv")
    u = matmul(x16, ab_w_in, 0, F32, col0=3 * d_attn, n=d_ssm, tm=2048, tn=512, name="ab_in_u")
    y_ab = attention(qkv.reshape(b, seq, 3 * d_attn), ab_rel_bias[0], n_heads_a, hb=2,
                     out_cols=d_attn + d_ssm, name="ab_attn")
    y_s = ssm_gelu(u.reshape(b, seq, d_ssm), ab_lam_re[0], ab_lam_im[0], ab_log_step[0], ab_b_re[0],
                   ab_b_im[0], ab_c_re[0], ab_c_im[0], ab_d_skip[0].reshape(-1), name="ab_ssm")
    y_ab = glu(y_s.reshape(t, d_ssm), ab_w_glu[0].astype(BF16), ab_b_glu[0],
               y_ab.reshape(t, d_attn + d_ssm), tm=512, name="ab_glu")
    y = matmul(y_ab, ab_w_out, 0, F32, col0=0, n=d, tm=1024, tn=512, resid=x32, alpha=ALPHA,
               name="ab_out")
    x32, x16 = layer_norm(y, ln_gain[0, 0], ln_bias[0, 0], tm=256, name="ab_ln")
    x32, x16 = _ffn(x32, x16, ffn_w_gate, ffn_w_up, ffn_w_down, ln_gain[0, 1], ln_bias[0, 1], 0)

    n_heads_c = c_a_log.shape[1]
    d_delta = n_heads_c * DK_DELTA
    cw_t = jnp.swapaxes(c_w_in, 1, 2)
    proj = matmul(x16, cw_t, 0, BF16, col0=0, n=4 * d_delta, tm=2048, tn=512, transposed=True,
                  name="c_in")
    braw = matmul(x16, cw_t[:, 4 * d_delta:], 0, F32, col0=0, n=2 * n_heads_c, tm=2048,
                  tn=2 * n_heads_c, transposed=True, name="c_in_small")
    o = gated_deltanet(proj.reshape(b, seq, 4 * d_delta), braw.reshape(b, seq, 2 * n_heads_c),
                       c_conv[0], c_a_log[0], c_dt_bias[0], c_norm_w[0], n_heads_c,
                       hb=16, tb=512, name="c_delta")
    y = matmul(o.reshape(t, d_delta), c_w_out, 0, F32, col0=0, n=d, tm=1024, tn=512, resid=x32,
               alpha=ALPHA, name="c_out")
    x32, x16 = layer_norm(y, ln_gain[1, 0], ln_bias[1, 0], tm=256, name="c_ln")
    x32, x16 = _ffn(x32, x16, ffn_w_gate, ffn_w_up, ffn_w_down, ln_gain[1, 1], ln_bias[1, 1], 1)
    return x32.reshape(b, seq, d).astype(x.dtype)
```

```python
import functools
import math

import jax
import jax.numpy as jnp
from jax import lax
from jax.experimental import pallas as pl
from jax.experimental.pallas import tpu as pltpu

F32 = jnp.float32
BF16 = jnp.bfloat16

CHUNK = 64
LEFT_CHUNKS = 8
DH_ATTN = 128
MAX_REL = 256
SSM_GROUP = 16
SSM_STATE = 64
DK_DELTA = 128
CONV_K = 4
DEPTH = 2
ALPHA = (2.0 * DEPTH) ** 0.25
LN_EPS = 1e-5
NORM_EPS = 1e-6
NEG_BIG = -1e30

V7X_VMEM_LIMIT_BYTES = 56 * 1024 * 1024
LANES = 128
SSM_BLOCK = 16
HI = lax.Precision.HIGHEST


def _cparams(n_axes, flags=None):
    return pltpu.CompilerParams(
        dimension_semantics=("arbitrary",) * n_axes,
        vmem_limit_bytes=V7X_VMEM_LIMIT_BYTES,
        flags=flags,
    )


def _sigmoid(x):
    return 0.5 * jnp.tanh(0.5 * x) + 0.5


def _mm_kernel(x_ref, w_ref, *rest, alpha, transposed):
    o_ref = rest[-1]
    w = w_ref[...].astype(BF16)
    dims = (((1,), (1,)), ((), ())) if transposed else (((1,), (0,)), ((), ()))
    y = lax.dot_general(x_ref[...], w, dims, preferred_element_type=F32)
    if len(rest) == 2:
        y = alpha * rest[0][...] + y
    o_ref[...] = y.astype(o_ref.dtype)


def matmul(x, w, layer, out_dtype, *, col0, n, tm, tn, name, transposed=False, resid=None, alpha=None):
    m, k = x.shape
    assert m % tm == 0 and n % tn == 0 and col0 % tn == 0
    j0 = col0 // tn
    if transposed:
        w_spec = pl.BlockSpec((None, tn, k), lambda i, j: (layer, j0 + j, 0))
    else:
        w_spec = pl.BlockSpec((None, k, tn), lambda i, j: (layer, 0, j0 + j))
    in_specs = [pl.BlockSpec((tm, k), lambda i, j: (i, 0), pipeline_mode=pl.Buffered(1)), w_spec]
    args = [x, w]
    if resid is not None:
        in_specs.append(pl.BlockSpec((tm, tn), lambda i, j: (i, j)))
        args.append(resid)
    return pl.pallas_call(
        functools.partial(_mm_kernel, alpha=alpha, transposed=transposed),
        grid=(m // tm, n // tn),
        in_specs=in_specs,
        out_specs=pl.BlockSpec((tm, tn), lambda i, j: (i, j)),
        out_shape=jax.ShapeDtypeStruct((m, n), out_dtype),
        compiler_params=_cparams(2),
        name=name,
    )(*args)


def _swiglu_kernel(x_ref, wg_ref, wu_ref, o_ref, *, n_valid_blocks):
    wg = wg_ref[...].astype(BF16)
    wu = wu_ref[...].astype(BF16)
    valid = pl.program_id(1) < n_valid_blocks
    half = x_ref.shape[0] // 2
    for r in (0, half):
        x = x_ref[r:r + half, :]
        g = jnp.dot(x, wg, preferred_element_type=F32)
        u = jnp.dot(x, wu, preferred_element_type=F32)
        h = g * _sigmoid(g) * u
        o_ref[r:r + half, :] = jnp.where(valid, h, 0.0).astype(o_ref.dtype)


def swiglu_up(x, wg, wu, layer, *, n_out, tm, tn, name):
    m, k = x.shape
    n = wg.shape[2]
    assert m % tm == 0 and n % tn == 0 and n_out % tn == 0 and n_out >= n
    nvb = n // tn
    wmap = lambda i, j: (layer, 0, jnp.minimum(j, nvb - 1))
    return pl.pallas_call(
        functools.partial(_swiglu_kernel, n_valid_blocks=nvb),
        grid=(m // tm, n_out // tn),
        in_specs=[
            pl.BlockSpec((tm, k), lambda i, j: (i, 0), pipeline_mode=pl.Buffered(1)),
            pl.BlockSpec((None, k, tn), wmap),
            pl.BlockSpec((None, k, tn), wmap),
        ],
        out_specs=pl.BlockSpec((tm, tn), lambda i, j: (i, j)),
        out_shape=jax.ShapeDtypeStruct((m, n_out), BF16),
        compiler_params=_cparams(2),
        name=name,
    )(x, wg, wu)


def _ln_kernel(y_ref, g_ref, b_ref, o32_ref, o16_ref):
    y = y_ref[...]
    mu = jnp.mean(y, axis=-1, keepdims=True)
    yc = y - mu
    var = jnp.mean(yc * yc, axis=-1, keepdims=True)
    x = yc * lax.rsqrt(var + LN_EPS) * g_ref[...] + b_ref[...]
    o32_ref[...] = x
    o16_ref[...] = x.astype(BF16)


def layer_norm(y, gain, bias, *, tm, name):
    m, d = y.shape
    assert m % tm == 0
    return pl.pallas_call(
        _ln_kernel,
        grid=(m // tm,),
        in_specs=[
            pl.BlockSpec((tm, d), lambda i: (i, 0)),
            pl.BlockSpec((1, d), lambda i: (0, 0)),
            pl.BlockSpec((1, d), lambda i: (0, 0)),
        ],
        out_specs=[
            pl.BlockSpec((tm, d), lambda i: (i, 0)),
            pl.BlockSpec((tm, d), lambda i: (i, 0)),
        ],
        out_shape=[jax.ShapeDtypeStruct((m, d), F32), jax.ShapeDtypeStruct((m, d), BF16)],
        compiler_params=_cparams(1),
        name=name,
    )(y, gain.reshape(1, d), bias.reshape(1, d))


ATTN_QB = 4 * CHUNK
ATTN_WIN = ATTN_QB + LEFT_CHUNKS * CHUNK
ATTN_NVAR = LEFT_CHUNKS * CHUNK // ATTN_QB + 1


def _attn_bias_tiles(rel_bias):
    n_heads = rel_bias.shape[0]
    period = ATTN_QB + ATTN_WIN
    m = jnp.arange(period)
    tiles = []
    for v in range(ATTN_NVAR):
        rel = v * ATTN_QB - jnp.where(m < ATTN_WIN, m, m - period)
        row = rel_bias.astype(F32)[:, jnp.clip(rel, -MAX_REL, MAX_REL) + MAX_REL]
        flat = jnp.tile(row, (1, ATTN_QB))[:, :ATTN_QB * (period - 1)]
        skew = flat.reshape(n_heads, ATTN_QB, period - 1)[:, :, :ATTN_WIN]
        tiles.append(skew)
    tiles = jnp.stack(tiles)
    i = jnp.arange(ATTN_QB)[None, :, None]
    j = jnp.arange(ATTN_WIN)[None, None, :]
    qpos = (jnp.arange(ATTN_NVAR) * ATTN_QB)[:, None, None] + i
    dchunk = qpos // CHUNK - j // CHUNK
    valid = (dchunk >= 0) & (dchunk <= LEFT_CHUNKS)
    return jnp.where(valid[:, None], tiles, NEG_BIG)


def _attn_kernel(q_ref, k_ref, v_ref, bias_ref, o_ref, *, hb, seq):
    scale = DH_ATTN ** -0.5
    nq = seq // ATTN_QB

    def body(qi, carry):
        qs = pl.multiple_of(qi * ATTN_QB, ATTN_QB)
        start = pl.multiple_of(jnp.maximum(qs - LEFT_CHUNKS * CHUNK, 0), ATTN_QB)
        var = jnp.minimum(qi, ATTN_NVAR - 1)
        heads = [slice(h * DH_ATTN, (h + 1) * DH_ATTN) for h in range(hb)]
        s_all = [lax.dot_general(q_ref[pl.ds(qs, ATTN_QB), c], k_ref[pl.ds(start, ATTN_WIN), c],
                                 (((1,), (1,)), ((), ())), preferred_element_type=F32)
                 for c in heads]
        p_all = []
        for h, s in enumerate(s_all):
            s = s * scale + bias_ref[var, h]
            m = jnp.max(s, axis=-1, keepdims=True)
            p = jnp.exp(s - m)
            p_all.append((p.astype(BF16), 1.0 / jnp.sum(p, axis=-1, keepdims=True)))
        for c, (p, inv_l) in zip(heads, p_all):
            o = jnp.dot(p, v_ref[pl.ds(start, ATTN_WIN), c], preferred_element_type=F32) * inv_l
            o_ref[pl.ds(qs, ATTN_QB), c] = o.astype(o_ref.dtype)
        return carry

    lax.fori_loop(0, nq, body, 0)


def attention(qkv, rel_bias, n_heads, *, hb, out_cols, name):
    b, seq, _ = qkv.shape
    assert n_heads % hb == 0 and seq % ATTN_QB == 0 and seq >= ATTN_WIN
    ng = n_heads // hb
    wcols = hb * DH_ATTN
    bias = _attn_bias_tiles(rel_bias)
    kern = functools.partial(_attn_kernel, hb=hb, seq=seq)
    return pl.pallas_call(
        kern,
        grid=(b, ng),
        in_specs=[
            pl.BlockSpec((None, seq, wcols), lambda bi, g: (bi, 0, g)),
            pl.BlockSpec((None, seq, wcols), lambda bi, g: (bi, 0, ng + g)),
            pl.BlockSpec((None, seq, wcols), lambda bi, g: (bi, 0, 2 * ng + g)),
            pl.BlockSpec((ATTN_NVAR, hb, ATTN_QB, ATTN_WIN), lambda bi, g: (0, g, 0, 0)),
        ],
        out_specs=pl.BlockSpec((None, seq, wcols), lambda bi, g: (bi, 0, g)),
        out_shape=jax.ShapeDtypeStruct((b, seq, out_cols), BF16),
        compiler_params=_cparams(2),
        name=name,
    )(qkv, qkv, qkv, bias)


SSM_TILE_GROUPS = LANES // SSM_GROUP
SSM_TILE_STATES = SSM_TILE_GROUPS * SSM_STATE


def _ssm_tables(lam_re, lam_im, log_step, b_re, b_im, c_re, c_im, n_scan_steps):
    g, p = lam_re.shape
    hg = b_re.shape[-1]
    tc = SSM_BLOCK
    tg = SSM_TILE_GROUPS
    nt = g // tg
    lam = lax.complex(jnp.minimum(lam_re.astype(F32), -1e-4), lam_im.astype(F32))
    step = jnp.exp(log_step.astype(F32))[:, None]
    lam_bar = jnp.exp(lam * step)
    b_bar = ((lam_bar - 1.0) / lam)[..., None] * lax.complex(b_re.astype(F32), b_im.astype(F32))
    pw = [jnp.ones_like(lam_bar)]
    for _ in range(tc):
        pw.append(pw[-1] * lam_bar)
    pw = jnp.stack(pw)
    xb = pw[:tc, :, :, None] * b_bar[None]
    xb_re, xb_im = jnp.real(xb), jnp.imag(xb)
    cr, ci = c_re.astype(F32), c_im.astype(F32)
    kmat = (jnp.einsum('ghp,tgpi->tghi', cr, xb_re, precision=HI)
            - jnp.einsum('ghp,tgpi->tghi', ci, xb_im, precision=HI))
    def diag_mask(rows_per_group, cols_per_group, col_groups_stride):
        r = jnp.arange(tg * rows_per_group)[:, None] // rows_per_group
        c = (jnp.arange(col_groups_stride)[None, :] // cols_per_group) % tg
        return r == c
    kc = jnp.transpose(kmat.reshape(tc, nt, tg, hg, hg), (1, 2, 4, 0, 3))
    kc = jnp.tile(kc.reshape(nt, tg * hg, tc, hg), (1, 1, 1, tg)).reshape(nt, tg * hg, tc * tg * hg)
    k_tab = jnp.where(diag_mask(hg, hg, tc * tg * hg), kc, 0.0).astype(BF16)
    k_tab = jnp.pad(k_tab, ((0, 0), (0, 0), (tg * hg, 0)))
    def f_part(xpart):
        xr = jnp.transpose(xpart[::-1].reshape(tc, nt, tg, p, hg), (1, 0, 2, 4, 3))
        xr = jnp.tile(xr.reshape(nt, tc, tg * hg, p), (1, 1, 1, tg))
        return jnp.where(diag_mask(hg, p, tg * p), xr, 0.0).astype(BF16)
    f_tab = jnp.concatenate([f_part(xb_re), f_part(xb_im)], axis=-1)
    gmat = lax.complex(cr, ci)[None] * pw[1:, :, None, :]
    def e_part(gpart):
        gr = jnp.transpose(gpart.reshape(tc, nt, tg, hg, p), (1, 2, 4, 0, 3))
        gr = jnp.tile(gr.reshape(nt, tg * p, tc, hg), (1, 1, 1, tg)).reshape(nt, tg * p, tc * tg * hg)
        return jnp.where(diag_mask(p, hg, tc * tg * hg), gr, 0.0).astype(BF16)
    e_tab = jnp.concatenate([e_part(jnp.real(gmat)), e_part(-jnp.imag(gmat))], axis=1)
    a = pw[tc]
    a1, a2 = [], []
    for _ in range(n_scan_steps):
        ar = jnp.real(a).reshape(nt, tg * p)
        ai = jnp.imag(a).reshape(nt, tg * p)
        a1.append(jnp.concatenate([ar, ar], axis=-1))
        a2.append(jnp.concatenate([-ai, ai], axis=-1))
        a = a * a
    pad = (-n_scan_steps) % 8
    a1 = jnp.pad(jnp.stack(a1, axis=1), ((0, 0), (0, pad), (0, 0)))
    a2 = jnp.pad(jnp.stack(a2, axis=1), ((0, 0), (0, pad), (0, 0)))
    return k_tab, f_tab, e_tab, a1, a2


def _gelu_tanh(x):
    c = math.sqrt(2.0 / math.pi)
    return 0.5 * x * (1.0 + jnp.tanh(c * (x + 0.044715 * (x * x * x))))


def _ssm_kernel(u_ref, k_ref, f_ref, e_ref, a1_ref, a2_ref, d_ref, y_ref, xs_ref, ky_ref, *, nc,
                nsteps):
    tc = SSM_BLOCK
    ns = SSM_TILE_STATES

    def frames(s):
        return u_ref[pl.ds(s, nc, stride=tc), :]

    sloc = None
    for s in range(tc):
        xs = frames(s).astype(BF16)
        xs_ref[s] = xs
        part = jnp.dot(xs, f_ref[s], preferred_element_type=F32)
        sloc = part if sloc is None else sloc + part
    pairs = list(range(0, tc, 2))

    def local_part(t):
        acc = None
        for s in range(t + 2):
            c0 = (t - s + 1) * LANES
            part = jnp.dot(xs_ref[s], k_ref[:, c0:c0 + 2 * LANES], preferred_element_type=F32)
            acc = part if acc is None else acc + part
        ky_ref[:, t * LANES:(t + 2) * LANES] = acc

    x = sloc
    for k in range(max(nsteps, len(pairs))):
        if k < nsteps:
            d = 1 << k
            sh = jnp.concatenate([jnp.zeros((d, 2 * ns), F32), x[:nc - d]], axis=0)
            sw = pltpu.roll(sh, ns, 1)
            x = x + a1_ref[k:k + 1, :] * sh + a2_ref[k:k + 1, :] * sw
        if k < len(pairs):
            local_part(pairs[k])
    sprev = jnp.concatenate([jnp.zeros((1, 2 * ns), F32), x[:nc - 1]], axis=0).astype(BF16)
    dskip = d_ref[...]
    for t in pairs:
        acc = ky_ref[:, t * LANES:(t + 2) * LANES] + jnp.dot(
            sprev, e_ref[:, t * LANES:(t + 2) * LANES], preferred_element_type=F32)
        for j in range(2):
            y = acc[:, j * LANES:(j + 1) * LANES] + dskip * frames(t + j)
            y_ref[pl.ds(t + j, nc, stride=tc), :] = _gelu_tanh(y)


def ssm_gelu(u, lam_re, lam_im, log_step, b_re, b_im, c_re, c_im, d_skip, *, name):
    b, seq, ds = u.shape
    g = lam_re.shape[0]
    tc = SSM_BLOCK
    nc = seq // tc
    assert seq % tc == 0 and ds == g * SSM_GROUP and ds % LANES == 0 and nc & (nc - 1) == 0
    assert lam_re.shape[1] == SSM_STATE
    nt = ds // LANES
    nsteps = nc.bit_length() - 1
    k_tab, f_tab, e_tab, a1, a2 = _ssm_tables(lam_re, lam_im, log_step, b_re, b_im, c_re, c_im, nsteps)
    ns2 = 2 * SSM_TILE_STATES
    sp = a1.shape[1]
    kern = functools.partial(_ssm_kernel, nc=nc, nsteps=nsteps)
    return pl.pallas_call(
        kern,
        grid=(nt, b),
        in_specs=[
            pl.BlockSpec((None, seq, LANES), lambda l, bi: (bi, 0, l)),
            pl.BlockSpec((None, LANES, (tc + 1) * LANES), lambda l, bi: (l, 0, 0)),
            pl.BlockSpec((None, tc, LANES, ns2), lambda l, bi: (l, 0, 0, 0)),
            pl.BlockSpec((None, ns2, tc * LANES), lambda l, bi: (l, 0, 0)),
            pl.BlockSpec((None, sp, ns2), lambda l, bi: (l, 0, 0)),
            pl.BlockSpec((None, sp, ns2), lambda l, bi: (l, 0, 0)),
            pl.BlockSpec((1, LANES), lambda l, bi: (0, l)),
        ],
        out_specs=pl.BlockSpec((None, seq, LANES), lambda l, bi: (bi, 0, l)),
        out_shape=jax.ShapeDtypeStruct((b, seq, ds), F32),
        scratch_shapes=[pltpu.VMEM((tc, nc, LANES), BF16), pltpu.VMEM((nc, tc * LANES), F32)],
        compiler_params=_cparams(2),
        name=name,
    )(u, k_tab, f_tab, e_tab, a1, a2, d_skip.reshape(1, ds).astype(F32))


def _glu_kernel(y_ref, w_ref, b_ref, dst_ref, o_ref):
    del dst_ref
    yb = y_ref[...]
    z = jnp.dot(yb.astype(BF16), w_ref[...], preferred_element_type=F32) + b_ref[...]
    o_ref[...] = (yb * _sigmoid(z)).astype(o_ref.dtype)


def glu(y, w_glu, b_glu, dst, *, tm, name):
    m, d = y.shape
    assert m % tm == 0 and dst.shape[0] == m and dst.shape[1] % d == 0 and dst.dtype == BF16
    col_blk = dst.shape[1] // d - 1
    return pl.pallas_call(
        _glu_kernel,
        grid=(m // tm,),
        in_specs=[
            pl.BlockSpec((tm, d), lambda i: (i, 0)),
            pl.BlockSpec((d, d), lambda i: (0, 0)),
            pl.BlockSpec((1, d), lambda i: (0, 0)),
            pl.BlockSpec(memory_space=pl.ANY),
        ],
        out_specs=pl.BlockSpec((tm, d), lambda i: (i, col_blk)),
        out_shape=jax.ShapeDtypeStruct(dst.shape, BF16),
        input_output_aliases={3: 0},
        compiler_params=_cparams(1),
        name=name,
    )(y, w_glu, b_glu.reshape(1, d).astype(F32), dst)


def _softplus(x):
    return jnp.maximum(x, 0.0) + jnp.log(1.0 + jnp.exp(-jnp.abs(x)))


def _cumsum_rows(x, n):
    rows = lax.broadcasted_iota(jnp.int32, x.shape, 0)
    d = 1
    while d < n:
        x = x + jnp.where(rows >= d, pltpu.roll(x, d, 0), 0.0)
        d *= 2
    return x


def _cumsum_lane_segments(x, seg):
    pos = lax.broadcasted_iota(jnp.int32, x.shape, 1) & (seg - 1)
    d = 1
    while d < seg:
        x = x + jnp.where(pos >= d, pltpu.roll(x, d, 1), 0.0)
        d *= 2
    return x


DELTA_PACK = 4
DELTA_ROWS = DELTA_PACK * CHUNK
DELTA_LEVELS = CHUNK.bit_length() - 1


def _delta_level_map():
    import numpy as np
    i = np.arange(DELTA_ROWS)[:, None]
    j = np.arange(DELTA_ROWS)[None, :]
    x = i ^ j
    lev = np.zeros_like(x)
    for a in range(1, 8):
        lev += (x >> a) != 0
    out = np.where((i > j) & ((i // CHUNK) == (j // CHUNK)), lev, 99)
    out = np.where(i == j, -1, out)
    return out.astype(np.int32)


def _delta_kernel(q_ref, k_ref, v_ref, gate_ref, bgc_ref, gr_ref, cwq_ref, cwk_ref, cwv_ref,
                  pc_ref, pr_ref, nw_ref, lev_ref, o_ref, s_ref, halo_ref, *, hb, tb):
    c = CHUNK
    dk = DK_DELTA
    pk = DELTA_PACK
    rows = DELTA_ROWS
    sub = hb // pk
    sw = pk * dk
    scale = dk ** -0.5
    n_chunks = tb // c
    ti = pl.program_id(2)

    @pl.when(ti == 0)
    def _():
        s_ref[...] = jnp.zeros_like(s_ref)
        halo_ref[...] = jnp.zeros_like(halo_ref)

    alog_c = pc_ref[0:1, :]
    dtb_c = pc_ref[1:2, :]
    alog_r = pr_ref[0:sub, :]
    dtb_r = pr_ref[sub:2 * sub, :]
    nw = nw_ref[...]

    def conv_silu(tail, cur, cw_ref, lanes):
        win = jnp.concatenate([tail, cur], axis=0)
        acc = (cw_ref[3:4, lanes] * win[8:8 + c] + cw_ref[2:3, lanes] * win[7:7 + c]
               + cw_ref[1:2, lanes] * win[6:6 + c] + cw_ref[0:1, lanes] * win[5:5 + c])
        return acc * _sigmoid(acc)

    def stack_heads(x):
        return jnp.concatenate([x[:, j * dk:(j + 1) * dk] for j in range(pk)], axis=0)

    def stack_cols(x, s, off):
        return jnp.concatenate(
            [x[:, off + pk * s + j:off + pk * s + j + 1] for j in range(pk)], axis=0)

    def chunk_stages(ch, get_tails):
        r0 = pl.multiple_of(ch * c, c)
        bg = bgc_ref[pl.ds(r0, c), :]
        beta_all = _sigmoid(bg)
        g_all = -jnp.exp(alog_c) * _softplus(bg + dtb_c)
        gc_all = _cumsum_rows(g_all, c)
        gl_all = jnp.broadcast_to(gc_all[c - 1:c, :], gc_all.shape)
        g_row = -jnp.exp(alog_r) * _softplus(gr_ref[ch] + dtb_r)
        gcr_all = _cumsum_lane_segments(g_row, c)
        lev = lev_ref[...]
        ri = lax.broadcasted_iota(jnp.int32, (rows, pk * dk), 0)
        cj = lax.broadcasted_iota(jnp.int32, (rows, pk * dk), 1)
        head_block = (ri // c) == (cj // dk)
        subs = [dict() for _ in range(sub)]

        def prep(s, st):
            tq, tk, tv = get_tails()
            lanes = slice(s * sw, (s + 1) * sw)
            curq = q_ref[pl.ds(r0, c), lanes].astype(F32)
            curk = k_ref[pl.ds(r0, c), lanes].astype(F32)
            curv = v_ref[pl.ds(r0, c), lanes].astype(F32)
            st["tails"] = (curq[c - 8:], curk[c - 8:], curv[c - 8:])
            q = stack_heads(conv_silu(tq[:, lanes], curq, cwq_ref, lanes))
            k = stack_heads(conv_silu(tk[:, lanes], curk, cwk_ref, lanes))
            v = stack_heads(conv_silu(tv[:, lanes], curv, cwv_ref, lanes))
            q = q * lax.rsqrt(jnp.sum(q * q, axis=-1, keepdims=True) + NORM_EPS)
            k = k * lax.rsqrt(jnp.sum(k * k, axis=-1, keepdims=True) + NORM_EPS)
            qs = q * scale
            beta = stack_cols(beta_all, s, 0)
            gc_col = stack_cols(gc_all, s, hb)
            gl_col = stack_cols(gl_all, s, hb)
            gc_row = gcr_all[s:s + 1, :]
            egc = jnp.exp(gc_col)
            kb = k * beta
            lhs = jnp.concatenate([kb, qs], axis=0).astype(BF16)
            kq = lax.dot_general(lhs, k.astype(BF16), (((1,), (1,)), ((), ())),
                                 preferred_element_type=F32)
            dec = jnp.exp(gc_col - gc_row)
            nm = kq[:rows] * dec
            st.update(
                nm=nm, qg=qs * egc,
                attn=jnp.where(lev < DELTA_LEVELS, kq[rows:] * dec, 0.0),
                x=jnp.concatenate([v * beta, kb * egc], axis=1),
                kd=(k * jnp.exp(gl_col - gc_col)).astype(BF16),
                t=jnp.where(lev == 0, -nm, jnp.where(lev == -1, 1.0, 0.0)))

        def level(a):
            def run(s, st):
                cm = jnp.where(lev == a, st["nm"], 0.0).astype(BF16)
                tb16 = st["t"].astype(BF16)
                pm = jnp.dot(tb16, cm, preferred_element_type=F32)
                st["t"] = st["t"] - jnp.dot(pm.astype(BF16), tb16, preferred_element_type=F32)
            return run

        def solve(s, st):
            t_off = jnp.where(lev == -1, 0.0, st["t"]).astype(BF16)
            x = st["x"]
            st["x"] = x + jnp.dot(t_off, x.astype(BF16), preferred_element_type=F32)

        def read_state(s, st):
            w_c = st["x"][:, dk:]
            qg = st["qg"]
            ws_parts, qs_parts = [], []
            for j in range(pk):
                h = pk * s + j
                rs = slice(j * c, (j + 1) * c)
                lhs2 = jnp.concatenate([w_c[rs], qg[rs]], axis=0).astype(BF16)
                ws = jnp.dot(lhs2, s_ref[h].astype(BF16), preferred_element_type=F32)
                ws_parts.append(ws[:c])
                qs_parts.append(ws[c:])
            st["v_new"] = st["x"][:, :dk] - jnp.concatenate(ws_parts, axis=0)
            st["qs"] = jnp.concatenate(qs_parts, axis=0)

        def outputs(s, st):
            st["out"] = st["qs"] + jnp.dot(st["attn"].astype(BF16), st["v_new"].astype(BF16),
                                           preferred_element_type=F32)
            vbd = jnp.where(head_block, jnp.concatenate([st["v_new"]] * pk, axis=1), 0.0)
            st["upd"] = lax.dot_general(st["kd"], vbd.astype(BF16), (((0,), (0,)), ((), ())),
                                        preferred_element_type=F32)

        def finish(s, st):
            out = st["out"]
            o = out * lax.rsqrt(jnp.mean(out * out, axis=-1, keepdims=True) + NORM_EPS) * nw
            gt = stack_heads(gate_ref[pl.ds(r0, c), s * sw:(s + 1) * sw].astype(F32))
            res = (o * (gt * _sigmoid(gt))).astype(o_ref.dtype)
            for j in range(pk):
                h = pk * s + j
                e_last = jnp.exp(gc_all[c - 1:c, hb + h:hb + h + 1])
                s_ref[h] = s_ref[h] * e_last + st["upd"][:, j * dk:(j + 1) * dk]
                o_ref[pl.ds(r0, c), h * dk:(h + 1) * dk] = res[j * c:(j + 1) * c]

        stages = [prep] + [level(a) for a in range(1, DELTA_LEVELS)] + [solve, read_state, outputs,
                                                                        finish]
        bound = [functools.partial(lambda f, s: f(s, subs[s]), f) for f in stages]
        tails_out = lambda: tuple(
            jnp.concatenate([st["tails"][i] for st in subs], axis=1) for i in range(3))
        return bound, tails_out

    def pair_body(it, tails):
        st_a, tails_a = chunk_stages(2 * it, lambda: tails)
        st_b, tails_b = chunk_stages(2 * it + 1, tails_a)
        n_lev = DELTA_LEVELS - 1
        prep_a, lev_a, rest_a = st_a[0], st_a[1:1 + n_lev], st_a[1 + n_lev:]
        prep_b, lev_b, rest_b = st_b[0], st_b[1:1 + n_lev], st_b[1 + n_lev:]
        for s in range(sub):
            prep_a(s)
        for a in range(n_lev):
            for s in range(sub):
                lev_a[a](s)
            for s in range(a * sub // n_lev, (a + 1) * sub // n_lev):
                prep_b(s)
        for s in range(sub):
            rest_a[0](s)
        for k, f_a in enumerate(rest_a[1:]):
            for s in range(sub):
                f_a(s)
                lev_b[k](s)
        for f_b in lev_b[len(rest_a) - 1:] + rest_b:
            for s in range(sub):
                f_b(s)
        return tails_b()

    assert n_chunks % 2 == 0
    tails0 = (halo_ref[0], halo_ref[1], halo_ref[2])
    tq, tk, tv = lax.fori_loop(0, n_chunks // 2, pair_body, tails0)
    halo_ref[0] = tq
    halo_ref[1] = tk
    halo_ref[2] = tv


def gated_deltanet(proj, braw, conv_w, a_log, dt_bias, norm_w, n_heads, *, hb, tb, name):
    b, seq, _ = proj.shape
    dk = DK_DELTA
    c = CHUNK
    pk = DELTA_PACK
    assert n_heads % hb == 0 and hb % pk == 0 and seq % tb == 0 and tb % c == 0
    ng = n_heads // hb
    sub = hb // pk
    w = hb * dk
    nct = seq // c
    br = braw.reshape(b, seq, 2, ng, hb)
    bgc = jnp.transpose(br, (0, 3, 1, 2, 4)).reshape(b, ng, seq, 2 * hb)
    araw = braw[..., n_heads:].reshape(b, nct, c, ng, sub, pk)
    gr = jnp.transpose(araw, (0, 3, 1, 4, 5, 2)).reshape(b, ng, nct, sub, pk * c)
    al = a_log.astype(F32).reshape(ng, hb)
    dt = dt_bias.astype(F32).reshape(ng, hb)
    zeros = jnp.zeros((ng, hb), F32)
    pc = jnp.stack([jnp.concatenate([zeros, al], axis=1),
                    jnp.concatenate([zeros, dt], axis=1)], axis=1)
    rep = lambda p: jnp.repeat(p.reshape(ng, sub, pk), c, axis=-1)
    pr = jnp.concatenate([rep(al), rep(dt)], axis=1)
    cw = conv_w.astype(F32)
    lev = jnp.asarray(_delta_level_map())
    rows = pk * c
    kern = functools.partial(_delta_kernel, hb=hb, tb=tb)
    return pl.pallas_call(
        kern,
        grid=(b, ng, seq // tb),
        in_specs=[
            pl.BlockSpec((None, tb, w), lambda bi, g, t: (bi, t, g)),
            pl.BlockSpec((None, tb, w), lambda bi, g, t: (bi, t, ng + g)),
            pl.BlockSpec((None, tb, w), lambda bi, g, t: (bi, t, 2 * ng + g)),
            pl.BlockSpec((None, tb, w), lambda bi, g, t: (bi, t, 3 * ng + g)),
            pl.BlockSpec((None, None, tb, 2 * hb), lambda bi, g, t: (bi, g, t, 0)),
            pl.BlockSpec((None, None, tb // c, sub, rows), lambda bi, g, t: (bi, g, t, 0, 0)),
            pl.BlockSpec((CONV_K, w), lambda bi, g, t: (0, g)),
            pl.BlockSpec((CONV_K, w), lambda bi, g, t: (0, ng + g)),
            pl.BlockSpec((CONV_K, w), lambda bi, g, t: (0, 2 * ng + g)),
            pl.BlockSpec((None, 2, 2 * hb), lambda bi, g, t: (g, 0, 0)),
            pl.BlockSpec((None, 2 * sub, rows), lambda bi, g, t: (g, 0, 0)),
            pl.BlockSpec((1, dk), lambda bi, g, t: (0, 0)),
            pl.BlockSpec((rows, rows), lambda bi, g, t: (0, 0)),
        ],
        out_specs=pl.BlockSpec((None, tb, w), lambda bi, g, t: (bi, t, g)),
        out_shape=jax.ShapeDtypeStruct((b, seq, n_heads * dk), BF16),
        scratch_shapes=[pltpu.VMEM((hb, dk, dk), F32), pltpu.VMEM((3, 8, w), F32)],
        compiler_params=_cparams(3),
        name=name,
    )(proj, proj, proj, proj, bgc, gr, cw, cw, cw, pc, pr, norm_w.astype(F32).reshape(1, dk), lev)


FFN_UP_TN = 256


def _ffn(x32, x16, w_gate, w_up, w_down, gain, bias, layer):
    dff = w_gate.shape[2]
    dff_pad = -(-dff // FFN_UP_TN) * FFN_UP_TN
    h = swiglu_up(x16, w_gate, w_up, layer, n_out=dff_pad, tm=2048, tn=FFN_UP_TN, name=f"ffn_up_{layer}")
    wd = jnp.pad(w_down[layer].astype(BF16), ((0, dff_pad - dff), (0, 0)))[None]
    y = matmul(h, wd, 0, F32, col0=0, n=wd.shape[2], tm=1024, tn=256, resid=x32, alpha=ALPHA,
               name=f"ffn_down_{layer}")
    return layer_norm(y, gain, bias, tm=256, name=f"ffn_ln_{layer}")


def kernel(x, ab_w_in, ab_rel_bias, ab_lam_re, ab_lam_im, ab_log_step, ab_b_re, ab_b_im, ab_c_re, ab_c_im, ab_d_skip, ab_w_glu, ab_b_glu, ab_w_out, c_w_in, c_conv, c_a_log, c_dt_bias, c_norm_w, c_w_out, ffn_w_gate, ffn_w_up, ffn_w_down, ln_gain, ln_bias):
    b, seq, d = x.shape
    t = b * seq
    x32 = x.reshape(t, d).astype(F32)
    x16 = x32.astype(BF16)

    n_heads_a = ab_rel_bias.shape[1]
    d_attn = n_heads_a * DH_ATTN
    d_ssm = ab_w_in.shape[2] - 3 * d_attn
    qkv = matmul(x16, ab_w_in, 0, BF16, col0=0, n=3 * d_attn, tm=2048, tn=512, name="ab_in_qkv")
    u = matmul(x16, ab_w_in, 0, F32, col0=3 * d_attn, n=d_ssm, tm=2048, tn=512, name="ab_in_u")
    y_ab = attention(qkv.reshape(b, seq, 3 * d_attn), ab_rel_bias[0], n_heads_a, hb=2,
                     out_cols=d_attn + d_ssm, name="ab_attn")
    y_s = ssm_gelu(u.reshape(b, seq, d_ssm), ab_lam_re[0], ab_lam_im[0], ab_log_step[0], ab_b_re[0],
                   ab_b_im[0], ab_c_re[0], ab_c_im[0], ab_d_skip[0].reshape(-1), name="ab_ssm")
    y_ab = glu(y_s.reshape(t, d_ssm), ab_w_glu[0].astype(BF16), ab_b_glu[0],
               y_ab.reshape(t, d_attn + d_ssm), tm=512, name="ab_glu")
    y = matmul(y_ab, ab_w_out, 0, F32, col0=0, n=d, tm=1024, tn=512, resid=x32, alpha=ALPHA,
               name="ab_out")
    x32, x16 = layer_norm(y, ln_gain[0, 0], ln_bias[0, 0], tm=256, name="ab_ln")
    x32, x16 = _ffn(x32, x16, ffn_w_gate, ffn_w_up, ffn_w_down, ln_gain[0, 1], ln_bias[0, 1], 0)

    n_heads_c = c_a_log.shape[1]
    d_delta = n_heads_c * DK_DELTA
    cw_t = jnp.swapaxes(c_w_in, 1, 2)
    proj = matmul(x16, cw_t, 0, BF16, col0=0, n=4 * d_delta, tm=2048, tn=512, transposed=True,
                  name="c_in")
    braw = matmul(x16, cw_t[:, 4 * d_delta:], 0, F32, col0=0, n=2 * n_heads_c, tm=2048,
                  tn=2 * n_heads_c, transposed=True, name="c_in_small")
    o = gated_deltanet(proj.reshape(b, seq, 4 * d_delta), braw.reshape(b, seq, 2 * n_heads_c),
                       c_conv[0], c_a_log[0], c_dt_bias[0], c_norm_w[0], n_heads_c,
                       hb=16, tb=512, name="c_delta")
    y = matmul(o.reshape(t, d_delta), c_w_out, 0, F32, col0=0, n=d, tm=1024, tn=512, resid=x32,
               alpha=ALPHA, name="c_out")
    x32, x16 = layer_norm(y, ln_gain[1, 0], ln_bias[1, 0], tm=256, name="c_ln")
    x32, x16 = _ffn(x32, x16, ffn_w_gate, ffn_w_up, ffn_w_down, ln_gain[1, 1], ln_bias[1, 1], 1)
    return x32.reshape(b, seq, d).astype(x.dtype)
```

```python
import functools
import math

import jax
import jax.numpy as jnp
from jax import lax
from jax.experimental import pallas as pl
from jax.experimental.pallas import tpu as pltpu

F32 = jnp.float32
BF16 = jnp.bfloat16

CHUNK = 64
LEFT_CHUNKS = 8
DH_ATTN = 128
MAX_REL = 256
SSM_GROUP = 16
SSM_STATE = 64
DK_DELTA = 128
CONV_K = 4
DEPTH = 2
ALPHA = (2.0 * DEPTH) ** 0.25
LN_EPS = 1e-5
NORM_EPS = 1e-6
NEG_BIG = -1e30

V7X_VMEM_LIMIT_BYTES = 56 * 1024 * 1024
LANES = 128
SSM_BLOCK = 16
HI = lax.Precision.HIGHEST


def _cparams(n_axes, flags=None):
    return pltpu.CompilerParams(
        dimension_semantics=("arbitrary",) * n_axes,
        vmem_limit_bytes=V7X_VMEM_LIMIT_BYTES,
        flags=flags,
    )


def _sigmoid(x):
    return 0.5 * jnp.tanh(0.5 * x) + 0.5


def _mm_kernel(x_ref, w_ref, *rest, alpha, transposed):
    o_ref = rest[-1]
    w = w_ref[...].astype(BF16)
    dims = (((1,), (1,)), ((), ())) if transposed else (((1,), (0,)), ((), ()))
    y = lax.dot_general(x_ref[...], w, dims, preferred_element_type=F32)
    if len(rest) == 2:
        y = alpha * rest[0][...] + y
    o_ref[...] = y.astype(o_ref.dtype)


def matmul(x, w, layer, out_dtype, *, col0, n, tm, tn, name, transposed=False, resid=None, alpha=None,
           x_buffers=1):
    m, k = x.shape
    assert m % tm == 0 and n % tn == 0 and col0 % tn == 0
    j0 = col0 // tn
    if transposed:
        w_spec = pl.BlockSpec((None, tn, k), lambda i, j: (layer, j0 + j, 0))
    else:
        w_spec = pl.BlockSpec((None, k, tn), lambda i, j: (layer, 0, j0 + j))
    in_specs = [pl.BlockSpec((tm, k), lambda i, j: (i, 0), pipeline_mode=pl.Buffered(x_buffers)),
                w_spec]
    args = [x, w]
    if resid is not None:
        in_specs.append(pl.BlockSpec((tm, tn), lambda i, j: (i, j)))
        args.append(resid)
    return pl.pallas_call(
        functools.partial(_mm_kernel, alpha=alpha, transposed=transposed),
        grid=(m // tm, n // tn),
        in_specs=in_specs,
        out_specs=pl.BlockSpec((tm, tn), lambda i, j: (i, j)),
        out_shape=jax.ShapeDtypeStruct((m, n), out_dtype),
        compiler_params=_cparams(2),
        name=name,
    )(*args)


def _swiglu_kernel(x_ref, wg_ref, wu_ref, o_ref, *, n_valid_blocks):
    wg = wg_ref[...].astype(BF16)
    wu = wu_ref[...].astype(BF16)
    valid = pl.program_id(1) < n_valid_blocks
    half = x_ref.shape[0] // 2
    for r in (0, half):
        x = x_ref[r:r + half, :]
        g = jnp.dot(x, wg, preferred_element_type=F32)
        u = jnp.dot(x, wu, preferred_element_type=F32)
        h = g * _sigmoid(g) * u
        o_ref[r:r + half, :] = jnp.where(valid, h, 0.0).astype(o_ref.dtype)


def swiglu_up(x, wg, wu, layer, *, n_out, tm, tn, name):
    m, k = x.shape
    n = wg.shape[2]
    assert m % tm == 0 and n % tn == 0 and n_out % tn == 0 and n_out >= n
    nvb = n // tn
    wmap = lambda i, j: (layer, 0, jnp.minimum(j, nvb - 1))
    return pl.pallas_call(
        functools.partial(_swiglu_kernel, n_valid_blocks=nvb),
        grid=(m // tm, n_out // tn),
        in_specs=[
            pl.BlockSpec((tm, k), lambda i, j: (i, 0), pipeline_mode=pl.Buffered(1)),
            pl.BlockSpec((None, k, tn), wmap),
            pl.BlockSpec((None, k, tn), wmap),
        ],
        out_specs=pl.BlockSpec((tm, tn), lambda i, j: (i, j)),
        out_shape=jax.ShapeDtypeStruct((m, n_out), BF16),
        compiler_params=_cparams(2),
        name=name,
    )(x, wg, wu)


def _ln_kernel(y_ref, g_ref, b_ref, o32_ref, *maybe_o16_ref):
    y = y_ref[...]
    mu = jnp.mean(y, axis=-1, keepdims=True)
    yc = y - mu
    var = jnp.mean(yc * yc, axis=-1, keepdims=True)
    x = yc * lax.rsqrt(var + LN_EPS) * g_ref[...] + b_ref[...]
    o32_ref[...] = x
    for o16_ref in maybe_o16_ref:
        o16_ref[...] = x.astype(BF16)


def layer_norm(y, gain, bias, *, tm, name, with_bf16=True):
    m, d = y.shape
    assert m % tm == 0
    row_spec = pl.BlockSpec((tm, d), lambda i: (i, 0))
    dtypes = [F32, BF16] if with_bf16 else [F32]
    return pl.pallas_call(
        _ln_kernel,
        grid=(m // tm,),
        in_specs=[
            row_spec,
            pl.BlockSpec((1, d), lambda i: (0, 0)),
            pl.BlockSpec((1, d), lambda i: (0, 0)),
        ],
        out_specs=[row_spec] * len(dtypes),
        out_shape=[jax.ShapeDtypeStruct((m, d), dt) for dt in dtypes],
        compiler_params=_cparams(1),
        name=name,
    )(y, gain.reshape(1, d), bias.reshape(1, d))


ATTN_QB = 4 * CHUNK
ATTN_WIN = ATTN_QB + LEFT_CHUNKS * CHUNK
ATTN_NVAR = LEFT_CHUNKS * CHUNK // ATTN_QB + 1


def _attn_bias_tiles(rel_bias):
    n_heads = rel_bias.shape[0]
    period = ATTN_QB + ATTN_WIN
    m = jnp.arange(period)
    tiles = []
    for v in range(ATTN_NVAR):
        rel = v * ATTN_QB - jnp.where(m < ATTN_WIN, m, m - period)
        row = rel_bias.astype(F32)[:, jnp.clip(rel, -MAX_REL, MAX_REL) + MAX_REL]
        flat = jnp.tile(row, (1, ATTN_QB))[:, :ATTN_QB * (period - 1)]
        skew = flat.reshape(n_heads, ATTN_QB, period - 1)[:, :, :ATTN_WIN]
        tiles.append(skew)
    tiles = jnp.stack(tiles)
    i = jnp.arange(ATTN_QB)[None, :, None]
    j = jnp.arange(ATTN_WIN)[None, None, :]
    qpos = (jnp.arange(ATTN_NVAR) * ATTN_QB)[:, None, None] + i
    dchunk = qpos // CHUNK - j // CHUNK
    valid = (dchunk >= 0) & (dchunk <= LEFT_CHUNKS)
    return jnp.where(valid[:, None], tiles, NEG_BIG)


def _attn_kernel(q_ref, k_ref, v_ref, bias_ref, o_ref, *, hb, seq):
    scale = DH_ATTN ** -0.5
    nq = seq // ATTN_QB

    def body(qi, carry):
        qs = pl.multiple_of(qi * ATTN_QB, ATTN_QB)
        start = pl.multiple_of(jnp.maximum(qs - LEFT_CHUNKS * CHUNK, 0), ATTN_QB)
        var = jnp.minimum(qi, ATTN_NVAR - 1)
        heads = [slice(h * DH_ATTN, (h + 1) * DH_ATTN) for h in range(hb)]
        s_all = [lax.dot_general(q_ref[pl.ds(qs, ATTN_QB), c], k_ref[pl.ds(start, ATTN_WIN), c],
                                 (((1,), (1,)), ((), ())), preferred_element_type=F32)
                 for c in heads]
        p_all = []
        for h, s in enumerate(s_all):
            s = s * scale + bias_ref[var, h]
            m = jnp.max(s, axis=-1, keepdims=True)
            p = jnp.exp(s - m)
            p_all.append((p.astype(BF16), 1.0 / jnp.sum(p, axis=-1, keepdims=True)))
        for c, (p, inv_l) in zip(heads, p_all):
            o = jnp.dot(p, v_ref[pl.ds(start, ATTN_WIN), c], preferred_element_type=F32) * inv_l
            o_ref[pl.ds(qs, ATTN_QB), c] = o.astype(o_ref.dtype)
        return carry

    lax.fori_loop(0, nq, body, 0)


def attention(qkv, rel_bias, n_heads, *, hb, out_cols, name):
    b, seq, _ = qkv.shape
    assert n_heads % hb == 0 and seq % ATTN_QB == 0 and seq >= ATTN_WIN
    ng = n_heads // hb
    wcols = hb * DH_ATTN
    bias = _attn_bias_tiles(rel_bias)
    kern = functools.partial(_attn_kernel, hb=hb, seq=seq)
    return pl.pallas_call(
        kern,
        grid=(b, ng),
        in_specs=[
            pl.BlockSpec((None, seq, wcols), lambda bi, g: (bi, 0, g)),
            pl.BlockSpec((None, seq, wcols), lambda bi, g: (bi, 0, ng + g)),
            pl.BlockSpec((None, seq, wcols), lambda bi, g: (bi, 0, 2 * ng + g)),
            pl.BlockSpec((ATTN_NVAR, hb, ATTN_QB, ATTN_WIN), lambda bi, g: (0, g, 0, 0)),
        ],
        out_specs=pl.BlockSpec((None, seq, wcols), lambda bi, g: (bi, 0, g)),
        out_shape=jax.ShapeDtypeStruct((b, seq, out_cols), BF16),
        compiler_params=_cparams(2),
        name=name,
    )(qkv, qkv, qkv, bias)


SSM_TILE_GROUPS = LANES // SSM_GROUP
SSM_TILE_STATES = SSM_TILE_GROUPS * SSM_STATE


def _ssm_tables(lam_re, lam_im, log_step, b_re, b_im, c_re, c_im, n_scan_steps):
    g, p = lam_re.shape
    hg = b_re.shape[-1]
    tc = SSM_BLOCK
    tg = SSM_TILE_GROUPS
    nt = g // tg
    lam = lax.complex(jnp.minimum(lam_re.astype(F32), -1e-4), lam_im.astype(F32))
    step = jnp.exp(log_step.astype(F32))[:, None]
    lam_bar = jnp.exp(lam * step)
    b_bar = ((lam_bar - 1.0) / lam)[..., None] * lax.complex(b_re.astype(F32), b_im.astype(F32))
    pw = [jnp.ones_like(lam_bar)]
    for _ in range(tc):
        pw.append(pw[-1] * lam_bar)
    pw = jnp.stack(pw)
    xb = pw[:tc, :, :, None] * b_bar[None]
    xb_re, xb_im = jnp.real(xb), jnp.imag(xb)
    cr, ci = c_re.astype(F32), c_im.astype(F32)
    kmat = (jnp.einsum('ghp,tgpi->tghi', cr, xb_re, precision=HI)
            - jnp.einsum('ghp,tgpi->tghi', ci, xb_im, precision=HI))
    def diag_mask(rows_per_group, cols_per_group, col_groups_stride):
        r = jnp.arange(tg * rows_per_group)[:, None] // rows_per_group
        c = (jnp.arange(col_groups_stride)[None, :] // cols_per_group) % tg
        return r == c
    kc = jnp.transpose(kmat.reshape(tc, nt, tg, hg, hg), (1, 2, 4, 0, 3))
    kc = jnp.tile(kc.reshape(nt, tg * hg, tc, hg), (1, 1, 1, tg)).reshape(nt, tg * hg, tc * tg * hg)
    k_tab = jnp.where(diag_mask(hg, hg, tc * tg * hg), kc, 0.0).astype(BF16)
    k_tab = jnp.pad(k_tab, ((0, 0), (0, 0), (tg * hg, 0)))
    def f_part(xpart):
        xr = jnp.transpose(xpart[::-1].reshape(tc, nt, tg, p, hg), (1, 0, 2, 4, 3))
        xr = jnp.tile(xr.reshape(nt, tc, tg * hg, p), (1, 1, 1, tg))
        return jnp.where(diag_mask(hg, p, tg * p), xr, 0.0).astype(BF16)
    f_tab = jnp.concatenate([f_part(xb_re), f_part(xb_im)], axis=-1)
    gmat = lax.complex(cr, ci)[None] * pw[1:, :, None, :]
    def e_part(gpart):
        gr = jnp.transpose(gpart.reshape(tc, nt, tg, hg, p), (1, 2, 4, 0, 3))
        gr = jnp.tile(gr.reshape(nt, tg * p, tc, hg), (1, 1, 1, tg)).reshape(nt, tg * p, tc * tg * hg)
        return jnp.where(diag_mask(p, hg, tc * tg * hg), gr, 0.0).astype(BF16)
    e_tab = jnp.concatenate([e_part(jnp.real(gmat)), e_part(-jnp.imag(gmat))], axis=1)
    a = pw[tc]
    a1, a2 = [], []
    for _ in range(n_scan_steps):
        ar = jnp.real(a).reshape(nt, tg * p)
        ai = jnp.imag(a).reshape(nt, tg * p)
        a1.append(jnp.concatenate([ar, ar], axis=-1))
        a2.append(jnp.concatenate([-ai, ai], axis=-1))
        a = a * a
    pad = (-n_scan_steps) % 8
    a1 = jnp.pad(jnp.stack(a1, axis=1), ((0, 0), (0, pad), (0, 0)))
    a2 = jnp.pad(jnp.stack(a2, axis=1), ((0, 0), (0, pad), (0, 0)))
    return k_tab, f_tab, e_tab, a1, a2


def _gelu_tanh(x):
    c = math.sqrt(2.0 / math.pi)
    return 0.5 * x * (1.0 + jnp.tanh(c * (x + 0.044715 * (x * x * x))))


def _ssm_kernel(u_ref, k_ref, f_ref, e_ref, a1_ref, a2_ref, d_ref, y_ref, xs_ref, ky_ref, *, nc,
                nsteps):
    tc = SSM_BLOCK
    ns = SSM_TILE_STATES

    def frames(s):
        return u_ref[pl.ds(s, nc, stride=tc), :]

    sloc = None
    for s in range(tc):
        xs = frames(s).astype(BF16)
        xs_ref[s] = xs
        part = jnp.dot(xs, f_ref[s], preferred_element_type=F32)
        sloc = part if sloc is None else sloc + part
    pairs = list(range(0, tc, 2))

    def local_part(t):
        acc = None
        for s in range(t + 2):
            c0 = (t - s + 1) * LANES
            part = jnp.dot(xs_ref[s], k_ref[:, c0:c0 + 2 * LANES], preferred_element_type=F32)
            acc = part if acc is None else acc + part
        ky_ref[:, t * LANES:(t + 2) * LANES] = acc

    x = sloc
    for k in range(max(nsteps, len(pairs))):
        if k < nsteps:
            d = 1 << k
            sh = jnp.concatenate([jnp.zeros((d, 2 * ns), F32), x[:nc - d]], axis=0)
            sw = pltpu.roll(sh, ns, 1)
            x = x + a1_ref[k:k + 1, :] * sh + a2_ref[k:k + 1, :] * sw
        if k < len(pairs):
            local_part(pairs[k])
    sprev = jnp.concatenate([jnp.zeros((1, 2 * ns), F32), x[:nc - 1]], axis=0).astype(BF16)
    dskip = d_ref[...]
    for t in pairs:
        acc = ky_ref[:, t * LANES:(t + 2) * LANES] + jnp.dot(
            sprev, e_ref[:, t * LANES:(t + 2) * LANES], preferred_element_type=F32)
        for j in range(2):
            y = acc[:, j * LANES:(j + 1) * LANES] + dskip * frames(t + j)
            y_ref[pl.ds(t + j, nc, stride=tc), :] = _gelu_tanh(y)


def ssm_gelu(u, lam_re, lam_im, log_step, b_re, b_im, c_re, c_im, d_skip, *, name):
    b, seq, ds = u.shape
    g = lam_re.shape[0]
    tc = SSM_BLOCK
    nc = seq // tc
    assert seq % tc == 0 and ds == g * SSM_GROUP and ds % LANES == 0 and nc & (nc - 1) == 0
    assert lam_re.shape[1] == SSM_STATE
    nt = ds // LANES
    nsteps = nc.bit_length() - 1
    k_tab, f_tab, e_tab, a1, a2 = _ssm_tables(lam_re, lam_im, log_step, b_re, b_im, c_re, c_im, nsteps)
    ns2 = 2 * SSM_TILE_STATES
    sp = a1.shape[1]
    kern = functools.partial(_ssm_kernel, nc=nc, nsteps=nsteps)
    return pl.pallas_call(
        kern,
        grid=(nt, b),
        in_specs=[
            pl.BlockSpec((None, seq, LANES), lambda l, bi: (bi, 0, l)),
            pl.BlockSpec((None, LANES, (tc + 1) * LANES), lambda l, bi: (l, 0, 0)),
            pl.BlockSpec((None, tc, LANES, ns2), lambda l, bi: (l, 0, 0, 0)),
            pl.BlockSpec((None, ns2, tc * LANES), lambda l, bi: (l, 0, 0)),
            pl.BlockSpec((None, sp, ns2), lambda l, bi: (l, 0, 0)),
            pl.BlockSpec((None, sp, ns2), lambda l, bi: (l, 0, 0)),
            pl.BlockSpec((1, LANES), lambda l, bi: (0, l)),
        ],
        out_specs=pl.BlockSpec((None, seq, LANES), lambda l, bi: (bi, 0, l)),
        out_shape=jax.ShapeDtypeStruct((b, seq, ds), F32),
        scratch_shapes=[pltpu.VMEM((tc, nc, LANES), BF16), pltpu.VMEM((nc, tc * LANES), F32)],
        compiler_params=_cparams(2),
        name=name,
    )(u, k_tab, f_tab, e_tab, a1, a2, d_skip.reshape(1, ds).astype(F32))


def _glu_kernel(y_ref, w_ref, b_ref, dst_ref, o_ref):
    del dst_ref
    yb = y_ref[...]
    z = jnp.dot(yb.astype(BF16), w_ref[...], preferred_element_type=F32) + b_ref[...]
    o_ref[...] = (yb * _sigmoid(z)).astype(o_ref.dtype)


def glu(y, w_glu, b_glu, dst, *, tm, name):
    m, d = y.shape
    assert m % tm == 0 and dst.shape[0] == m and dst.shape[1] % d == 0 and dst.dtype == BF16
    col_blk = dst.shape[1] // d - 1
    return pl.pallas_call(
        _glu_kernel,
        grid=(m // tm,),
        in_specs=[
            pl.BlockSpec((tm, d), lambda i: (i, 0)),
            pl.BlockSpec((d, d), lambda i: (0, 0)),
            pl.BlockSpec((1, d), lambda i: (0, 0)),
            pl.BlockSpec(memory_space=pl.ANY),
        ],
        out_specs=pl.BlockSpec((tm, d), lambda i: (i, col_blk)),
        out_shape=jax.ShapeDtypeStruct(dst.shape, BF16),
        input_output_aliases={3: 0},
        compiler_params=_cparams(1),
        name=name,
    )(y, w_glu, b_glu.reshape(1, d).astype(F32), dst)


def _softplus(x):
    return jnp.maximum(x, 0.0) + jnp.log(1.0 + jnp.exp(-jnp.abs(x)))


def _cumsum_rows(x, n):
    rows = lax.broadcasted_iota(jnp.int32, x.shape, 0)
    d = 1
    while d < n:
        x = x + jnp.where(rows >= d, pltpu.roll(x, d, 0), 0.0)
        d *= 2
    return x


def _cumsum_lane_segments(x, seg):
    pos = lax.broadcasted_iota(jnp.int32, x.shape, 1) & (seg - 1)
    d = 1
    while d < seg:
        x = x + jnp.where(pos >= d, pltpu.roll(x, d, 1), 0.0)
        d *= 2
    return x


DELTA_PACK = 4
DELTA_ROWS = DELTA_PACK * CHUNK
DELTA_LEVELS = CHUNK.bit_length() - 1


def _delta_level_map():
    import numpy as np
    i = np.arange(DELTA_ROWS)[:, None]
    j = np.arange(DELTA_ROWS)[None, :]
    x = i ^ j
    lev = np.zeros_like(x)
    for a in range(1, 8):
        lev += (x >> a) != 0
    out = np.where((i > j) & ((i // CHUNK) == (j // CHUNK)), lev, 99)
    out = np.where(i == j, -1, out)
    return out.astype(np.int32)


def _delta_kernel(q_ref, k_ref, v_ref, gate_ref, bgc_ref, gr_ref, cwq_ref, cwk_ref, cwv_ref,
                  pc_ref, pr_ref, nw_ref, lev_ref, o_ref, s_ref, halo_ref, *, hb, tb):
    c = CHUNK
    dk = DK_DELTA
    pk = DELTA_PACK
    rows = DELTA_ROWS
    sub = hb // pk
    sw = pk * dk
    scale = dk ** -0.5
    n_chunks = tb // c
    ti = pl.program_id(2)

    @pl.when(ti == 0)
    def _():
        s_ref[...] = jnp.zeros_like(s_ref)
        halo_ref[...] = jnp.zeros_like(halo_ref)

    alog_c = pc_ref[0:1, :]
    dtb_c = pc_ref[1:2, :]
    alog_r = pr_ref[0:sub, :]
    dtb_r = pr_ref[sub:2 * sub, :]
    nw = nw_ref[...]

    def conv_silu(tail, cur, cw_ref, lanes):
        win = jnp.concatenate([tail, cur], axis=0)
        acc = (cw_ref[3:4, lanes] * win[8:8 + c] + cw_ref[2:3, lanes] * win[7:7 + c]
               + cw_ref[1:2, lanes] * win[6:6 + c] + cw_ref[0:1, lanes] * win[5:5 + c])
        return acc * _sigmoid(acc)

    def stack_heads(x):
        return jnp.concatenate([x[:, j * dk:(j + 1) * dk] for j in range(pk)], axis=0)

    def stack_cols(x, s, off):
        return jnp.concatenate(
            [x[:, off + pk * s + j:off + pk * s + j + 1] for j in range(pk)], axis=0)

    def chunk_stages(ch, get_tails):
        r0 = pl.multiple_of(ch * c, c)
        bg = bgc_ref[pl.ds(r0, c), :]
        beta_all = _sigmoid(bg)
        g_all = -jnp.exp(alog_c) * _softplus(bg + dtb_c)
        gc_all = _cumsum_rows(g_all, c)
        gl_all = jnp.broadcast_to(gc_all[c - 1:c, :], gc_all.shape)
        g_row = -jnp.exp(alog_r) * _softplus(gr_ref[ch] + dtb_r)
        gcr_all = _cumsum_lane_segments(g_row, c)
        lev = lev_ref[...]
        ri = lax.broadcasted_iota(jnp.int32, (rows, pk * dk), 0)
        cj = lax.broadcasted_iota(jnp.int32, (rows, pk * dk), 1)
        head_block = (ri // c) == (cj // dk)
        subs = [dict() for _ in range(sub)]

        def prep(s, st):
            tq, tk, tv = get_tails()
            lanes = slice(s * sw, (s + 1) * sw)
            curq = q_ref[pl.ds(r0, c), lanes].astype(F32)
            curk = k_ref[pl.ds(r0, c), lanes].astype(F32)
            curv = v_ref[pl.ds(r0, c), lanes].astype(F32)
            st["tails"] = (curq[c - 8:], curk[c - 8:], curv[c - 8:])
            q = stack_heads(conv_silu(tq[:, lanes], curq, cwq_ref, lanes))
            k = stack_heads(conv_silu(tk[:, lanes], curk, cwk_ref, lanes))
            v = stack_heads(conv_silu(tv[:, lanes], curv, cwv_ref, lanes))
            q = q * lax.rsqrt(jnp.sum(q * q, axis=-1, keepdims=True) + NORM_EPS)
            k = k * lax.rsqrt(jnp.sum(k * k, axis=-1, keepdims=True) + NORM_EPS)
            qs = q * scale
            beta = stack_cols(beta_all, s, 0)
            gc_col = stack_cols(gc_all, s, hb)
            gl_col = stack_cols(gl_all, s, hb)
            gc_row = gcr_all[s:s + 1, :]
            egc = jnp.exp(gc_col)
            kb = k * beta
            lhs = jnp.concatenate([kb, qs], axis=0).astype(BF16)
            kq = lax.dot_general(lhs, k.astype(BF16), (((1,), (1,)), ((), ())),
                                 preferred_element_type=F32)
            dec = jnp.exp(gc_col - gc_row)
            nm = kq[:rows] * dec
            st.update(
                nm=nm, qg=qs * egc,
                attn=jnp.where(lev < DELTA_LEVELS, kq[rows:] * dec, 0.0),
                x=jnp.concatenate([v * beta, kb * egc], axis=1),
                kd=(k * jnp.exp(gl_col - gc_col)).astype(BF16),
                t=jnp.where(lev == 0, -nm, jnp.where(lev == -1, 1.0, 0.0)))

        def level(a):
            def run(s, st):
                cm = jnp.where(lev == a, st["nm"], 0.0).astype(BF16)
                tb16 = st["t"].astype(BF16)
                pm = jnp.dot(tb16, cm, preferred_element_type=F32)
                st["t"] = st["t"] - jnp.dot(pm.astype(BF16), tb16, preferred_element_type=F32)
            return run

        def solve(s, st):
            t_off = jnp.where(lev == -1, 0.0, st["t"]).astype(BF16)
            x = st["x"]
            st["x"] = x + jnp.dot(t_off, x.astype(BF16), preferred_element_type=F32)

        def read_state(s, st):
            w_c = st["x"][:, dk:]
            qg = st["qg"]
            ws_parts, qs_parts = [], []
            for j in range(pk):
                h = pk * s + j
                rs = slice(j * c, (j + 1) * c)
                lhs2 = jnp.concatenate([w_c[rs], qg[rs]], axis=0).astype(BF16)
                ws = jnp.dot(lhs2, s_ref[h].astype(BF16), preferred_element_type=F32)
                ws_parts.append(ws[:c])
                qs_parts.append(ws[c:])
            st["v_new"] = st["x"][:, :dk] - jnp.concatenate(ws_parts, axis=0)
            st["qs"] = jnp.concatenate(qs_parts, axis=0)

        def outputs(s, st):
            st["out"] = st["qs"] + jnp.dot(st["attn"].astype(BF16), st["v_new"].astype(BF16),
                                           preferred_element_type=F32)
            vbd = jnp.where(head_block, jnp.concatenate([st["v_new"]] * pk, axis=1), 0.0)
            st["upd"] = lax.dot_general(st["kd"], vbd.astype(BF16), (((0,), (0,)), ((), ())),
                                        preferred_element_type=F32)

        def finish(s, st):
            out = st["out"]
            o = out * lax.rsqrt(jnp.mean(out * out, axis=-1, keepdims=True) + NORM_EPS) * nw
            gt = stack_heads(gate_ref[pl.ds(r0, c), s * sw:(s + 1) * sw].astype(F32))
            res = (o * (gt * _sigmoid(gt))).astype(o_ref.dtype)
            for j in range(pk):
                h = pk * s + j
                e_last = jnp.exp(gc_all[c - 1:c, hb + h:hb + h + 1])
                s_ref[h] = s_ref[h] * e_last + st["upd"][:, j * dk:(j + 1) * dk]
                o_ref[pl.ds(r0, c), h * dk:(h + 1) * dk] = res[j * c:(j + 1) * c]

        stages = [prep] + [level(a) for a in range(1, DELTA_LEVELS)] + [solve, read_state, outputs,
                                                                        finish]
        bound = [functools.partial(lambda f, s: f(s, subs[s]), f) for f in stages]
        tails_out = lambda: tuple(
            jnp.concatenate([st["tails"][i] for st in subs], axis=1) for i in range(3))
        return bound, tails_out

    def pair_body(it, tails):
        st_a, tails_a = chunk_stages(2 * it, lambda: tails)
        st_b, tails_b = chunk_stages(2 * it + 1, tails_a)
        n_lev = DELTA_LEVELS - 1
        prep_a, lev_a, rest_a = st_a[0], st_a[1:1 + n_lev], st_a[1 + n_lev:]
        prep_b, lev_b, rest_b = st_b[0], st_b[1:1 + n_lev], st_b[1 + n_lev:]
        for s in range(sub):
            prep_a(s)
        for a in range(n_lev):
            for s in range(sub):
                lev_a[a](s)
            for s in range(a * sub // n_lev, (a + 1) * sub // n_lev):
                prep_b(s)
        for s in range(sub):
            rest_a[0](s)
        for k, f_a in enumerate(rest_a[1:]):
            for s in range(sub):
                f_a(s)
                lev_b[k](s)
        for f_b in lev_b[len(rest_a) - 1:] + rest_b:
            for s in range(sub):
                f_b(s)
        return tails_b()

    assert n_chunks % 2 == 0
    tails0 = (halo_ref[0], halo_ref[1], halo_ref[2])
    tq, tk, tv = lax.fori_loop(0, n_chunks // 2, pair_body, tails0)
    halo_ref[0] = tq
    halo_ref[1] = tk
    halo_ref[2] = tv


def gated_deltanet(proj, braw, conv_w, a_log, dt_bias, norm_w, n_heads, *, hb, tb, name):
    b, seq, _ = proj.shape
    dk = DK_DELTA
    c = CHUNK
    pk = DELTA_PACK
    assert n_heads % hb == 0 and hb % pk == 0 and seq % tb == 0 and tb % c == 0
    ng = n_heads // hb
    sub = hb // pk
    w = hb * dk
    nct = seq // c
    br = braw.reshape(b, seq, 2, ng, hb)
    bgc = jnp.transpose(br, (0, 3, 1, 2, 4)).reshape(b, ng, seq, 2 * hb)
    araw = braw[..., n_heads:].reshape(b, nct, c, ng, sub, pk)
    gr = jnp.transpose(araw, (0, 3, 1, 4, 5, 2)).reshape(b, ng, nct, sub, pk * c)
    al = a_log.astype(F32).reshape(ng, hb)
    dt = dt_bias.astype(F32).reshape(ng, hb)
    zeros = jnp.zeros((ng, hb), F32)
    pc = jnp.stack([jnp.concatenate([zeros, al], axis=1),
                    jnp.concatenate([zeros, dt], axis=1)], axis=1)
    rep = lambda p: jnp.repeat(p.reshape(ng, sub, pk), c, axis=-1)
    pr = jnp.concatenate([rep(al), rep(dt)], axis=1)
    cw = conv_w.astype(F32)
    lev = jnp.asarray(_delta_level_map())
    rows = pk * c
    kern = functools.partial(_delta_kernel, hb=hb, tb=tb)
    return pl.pallas_call(
        kern,
        grid=(b, ng, seq // tb),
        in_specs=[
            pl.BlockSpec((None, tb, w), lambda bi, g, t: (bi, t, g)),
            pl.BlockSpec((None, tb, w), lambda bi, g, t: (bi, t, ng + g)),
            pl.BlockSpec((None, tb, w), lambda bi, g, t: (bi, t, 2 * ng + g)),
            pl.BlockSpec((None, tb, w), lambda bi, g, t: (bi, t, 3 * ng + g)),
            pl.BlockSpec((None, None, tb, 2 * hb), lambda bi, g, t: (bi, g, t, 0)),
            pl.BlockSpec((None, None, tb // c, sub, rows), lambda bi, g, t: (bi, g, t, 0, 0)),
            pl.BlockSpec((CONV_K, w), lambda bi, g, t: (0, g)),
            pl.BlockSpec((CONV_K, w), lambda bi, g, t: (0, ng + g)),
            pl.BlockSpec((CONV_K, w), lambda bi, g, t: (0, 2 * ng + g)),
            pl.BlockSpec((None, 2, 2 * hb), lambda bi, g, t: (g, 0, 0)),
            pl.BlockSpec((None, 2 * sub, rows), lambda bi, g, t: (g, 0, 0)),
            pl.BlockSpec((1, dk), lambda bi, g, t: (0, 0)),
            pl.BlockSpec((rows, rows), lambda bi, g, t: (0, 0)),
        ],
        out_specs=pl.BlockSpec((None, tb, w), lambda bi, g, t: (bi, t, g)),
        out_shape=jax.ShapeDtypeStruct((b, seq, n_heads * dk), BF16),
        scratch_shapes=[pltpu.VMEM((hb, dk, dk), F32), pltpu.VMEM((3, 8, w), F32)],
        compiler_params=_cparams(3),
        name=name,
    )(proj, proj, proj, proj, bgc, gr, cw, cw, cw, pc, pr, norm_w.astype(F32).reshape(1, dk), lev)


FFN_UP_TN = 256


def _ffn(x32, x16, w_gate, w_up, w_down, gain, bias, layer, last=False):
    dff = w_gate.shape[2]
    dff_pad = -(-dff // FFN_UP_TN) * FFN_UP_TN
    h = swiglu_up(x16, w_gate, w_up, layer, n_out=dff_pad, tm=2048, tn=FFN_UP_TN, name=f"ffn_up_{layer}")
    wd = jnp.pad(w_down[layer].astype(BF16), ((0, dff_pad - dff), (0, 0)))[None]
    y = matmul(h, wd, 0, F32, col0=0, n=wd.shape[2], tm=512, tn=512, resid=x32, alpha=ALPHA,
               x_buffers=2, name=f"ffn_down_{layer}")
    return layer_norm(y, gain, bias, tm=256, with_bf16=not last, name=f"ffn_ln_{layer}")


def kernel(x, ab_w_in, ab_rel_bias, ab_lam_re, ab_lam_im, ab_log_step, ab_b_re, ab_b_im, ab_c_re, ab_c_im, ab_d_skip, ab_w_glu, ab_b_glu, ab_w_out, c_w_in, c_conv, c_a_log, c_dt_bias, c_norm_w, c_w_out, ffn_w_gate, ffn_w_up, ffn_w_down, ln_gain, ln_bias):
    b, seq, d = x.shape
    t = b * seq
    x32 = x.reshape(t, d).astype(F32)
    x16 = x32.astype(BF16)

    n_heads_a = ab_rel_bias.shape[1]
    d_attn = n_heads_a * DH_ATTN
    d_ssm = ab_w_in.shape[2] - 3 * d_attn
    qkv = matmul(x16, ab_w_in, 0, BF16, col0=0, n=3 * d_attn, tm=2048, tn=512, name="ab_in_qkv")
    u = matmul(x16, ab_w_in, 0, F32, col0=3 * d_attn, n=d_ssm, tm=2048, tn=512, name="ab_in_u")
    y_ab = attention(qkv.reshape(b, seq, 3 * d_attn), ab_rel_bias[0], n_heads_a, hb=2,
                     out_cols=d_attn + d_ssm, name="ab_attn")
    y_s = ssm_gelu(u.reshape(b, seq, d_ssm), ab_lam_re[0], ab_lam_im[0], ab_log_step[0], ab_b_re[0],
                   ab_b_im[0], ab_c_re[0], ab_c_im[0], ab_d_skip[0].reshape(-1), name="ab_ssm")
    y_ab = glu(y_s.reshape(t, d_ssm), ab_w_glu[0].astype(BF16), ab_b_glu[0],
               y_ab.reshape(t, d_attn + d_ssm), tm=512, name="ab_glu")
    y = matmul(y_ab, ab_w_out, 0, F32, col0=0, n=d, tm=1024, tn=512, resid=x32, alpha=ALPHA,
               x_buffers=2, name="ab_out")
    x32, x16 = layer_norm(y, ln_gain[0, 0], ln_bias[0, 0], tm=256, name="ab_ln")
    x32, x16 = _ffn(x32, x16, ffn_w_gate, ffn_w_up, ffn_w_down, ln_gain[0, 1], ln_bias[0, 1], 0)

    n_heads_c = c_a_log.shape[1]
    d_delta = n_heads_c * DK_DELTA
    cw_t = jnp.swapaxes(c_w_in, 1, 2)
    proj = matmul(x16, cw_t, 0, BF16, col0=0, n=4 * d_delta, tm=2048, tn=512, transposed=True,
                  name="c_in")
    braw = matmul(x16, cw_t[:, 4 * d_delta:], 0, F32, col0=0, n=2 * n_heads_c, tm=2048,
                  tn=2 * n_heads_c, transposed=True, name="c_in_small")
    o = gated_deltanet(proj.reshape(b, seq, 4 * d_delta), braw.reshape(b, seq, 2 * n_heads_c),
                       c_conv[0], c_a_log[0], c_dt_bias[0], c_norm_w[0], n_heads_c,
                       hb=16, tb=512, name="c_delta")
    y = matmul(o.reshape(t, d_delta), c_w_out, 0, F32, col0=0, n=d, tm=1024, tn=512, resid=x32,
               alpha=ALPHA, x_buffers=2, name="c_out")
    x32, x16 = layer_norm(y, ln_gain[1, 0], ln_bias[1, 0], tm=256, name="c_ln")
    (x32,) = _ffn(x32, x16, ffn_w_gate, ffn_w_up, ffn_w_down, ln_gain[1, 1], ln_bias[1, 1], 1,
                  last=True)
    return x32.reshape(b, seq, d).astype(x.dtype)
```

```python
import functools
import math

import jax
import jax.numpy as jnp
from jax import lax
from jax.experimental import pallas as pl
from jax.experimental.pallas import tpu as pltpu

F32 = jnp.float32
BF16 = jnp.bfloat16

CHUNK = 64
LEFT_CHUNKS = 8
DH_ATTN = 128
MAX_REL = 256
SSM_GROUP = 16
SSM_STATE = 64
DK_DELTA = 128
CONV_K = 4
DEPTH = 2
ALPHA = (2.0 * DEPTH) ** 0.25
LN_EPS = 1e-5
NORM_EPS = 1e-6
NEG_BIG = -1e30

V7X_VMEM_LIMIT_BYTES = 56 * 1024 * 1024
LANES = 128
SSM_BLOCK = 16
HI = lax.Precision.HIGHEST


def _cparams(n_axes, flags=None):
    return pltpu.CompilerParams(
        dimension_semantics=("arbitrary",) * n_axes,
        vmem_limit_bytes=V7X_VMEM_LIMIT_BYTES,
        flags=flags,
    )


def _sigmoid(x):
    return 0.5 * jnp.tanh(0.5 * x) + 0.5


def _mm_kernel(x_ref, w_ref, *rest, alpha, transposed):
    o_ref = rest[-1]
    w = w_ref[...].astype(BF16)
    dims = (((1,), (1,)), ((), ())) if transposed else (((1,), (0,)), ((), ()))
    y = lax.dot_general(x_ref[...], w, dims, preferred_element_type=F32)
    if len(rest) == 2:
        y = alpha * rest[0][...] + y
    o_ref[...] = y.astype(o_ref.dtype)


def matmul(x, w, layer, out_dtype, *, col0, n, tm, tn, name, transposed=False, resid=None, alpha=None,
           x_buffers=1):
    m, k = x.shape
    assert m % tm == 0 and n % tn == 0 and col0 % tn == 0
    j0 = col0 // tn
    if transposed:
        w_spec = pl.BlockSpec((None, tn, k), lambda i, j: (layer, j0 + j, 0))
    else:
        w_spec = pl.BlockSpec((None, k, tn), lambda i, j: (layer, 0, j0 + j))
    in_specs = [pl.BlockSpec((tm, k), lambda i, j: (i, 0), pipeline_mode=pl.Buffered(x_buffers)),
                w_spec]
    args = [x, w]
    if resid is not None:
        in_specs.append(pl.BlockSpec((tm, tn), lambda i, j: (i, j)))
        args.append(resid)
    return pl.pallas_call(
        functools.partial(_mm_kernel, alpha=alpha, transposed=transposed),
        grid=(m // tm, n // tn),
        in_specs=in_specs,
        out_specs=pl.BlockSpec((tm, tn), lambda i, j: (i, j)),
        out_shape=jax.ShapeDtypeStruct((m, n), out_dtype),
        compiler_params=_cparams(2),
        name=name,
    )(*args)


def _swiglu_kernel(x_ref, wg_ref, wu_ref, o_ref, *, n_valid_blocks):
    wg = wg_ref[...].astype(BF16)
    wu = wu_ref[...].astype(BF16)
    valid = pl.program_id(1) < n_valid_blocks
    half = x_ref.shape[0] // 2
    for r in (0, half):
        x = x_ref[r:r + half, :]
        g = jnp.dot(x, wg, preferred_element_type=F32)
        u = jnp.dot(x, wu, preferred_element_type=F32)
        h = g * _sigmoid(g) * u
        o_ref[r:r + half, :] = jnp.where(valid, h, 0.0).astype(o_ref.dtype)


def swiglu_up(x, wg, wu, layer, *, n_out, tm, tn, name):
    m, k = x.shape
    n = wg.shape[2]
    assert m % tm == 0 and n % tn == 0 and n_out % tn == 0 and n_out >= n
    nvb = n // tn
    wmap = lambda i, j: (layer, 0, jnp.minimum(j, nvb - 1))
    return pl.pallas_call(
        functools.partial(_swiglu_kernel, n_valid_blocks=nvb),
        grid=(m // tm, n_out // tn),
        in_specs=[
            pl.BlockSpec((tm, k), lambda i, j: (i, 0), pipeline_mode=pl.Buffered(1)),
            pl.BlockSpec((None, k, tn), wmap),
            pl.BlockSpec((None, k, tn), wmap),
        ],
        out_specs=pl.BlockSpec((tm, tn), lambda i, j: (i, j)),
        out_shape=jax.ShapeDtypeStruct((m, n_out), BF16),
        compiler_params=_cparams(2),
        name=name,
    )(x, wg, wu)


def _ln_kernel(y_ref, g_ref, b_ref, o32_ref, *maybe_o16_ref):
    y = y_ref[...]
    mu = jnp.mean(y, axis=-1, keepdims=True)
    yc = y - mu
    var = jnp.mean(yc * yc, axis=-1, keepdims=True)
    x = yc * lax.rsqrt(var + LN_EPS) * g_ref[...] + b_ref[...]
    o32_ref[...] = x
    for o16_ref in maybe_o16_ref:
        o16_ref[...] = x.astype(BF16)


def layer_norm(y, gain, bias, *, tm, name, with_bf16=True):
    m, d = y.shape
    assert m % tm == 0
    row_spec = pl.BlockSpec((tm, d), lambda i: (i, 0))
    dtypes = [F32, BF16] if with_bf16 else [F32]
    return pl.pallas_call(
        _ln_kernel,
        grid=(m // tm,),
        in_specs=[
            row_spec,
            pl.BlockSpec((1, d), lambda i: (0, 0)),
            pl.BlockSpec((1, d), lambda i: (0, 0)),
        ],
        out_specs=[row_spec] * len(dtypes),
        out_shape=[jax.ShapeDtypeStruct((m, d), dt) for dt in dtypes],
        compiler_params=_cparams(1),
        name=name,
    )(y, gain.reshape(1, d), bias.reshape(1, d))


ATTN_QB = 4 * CHUNK
ATTN_UNROLL = 4
ATTN_WIN = ATTN_QB + LEFT_CHUNKS * CHUNK
ATTN_NVAR = LEFT_CHUNKS * CHUNK // ATTN_QB + 1


def _attn_bias_tiles(rel_bias):
    n_heads = rel_bias.shape[0]
    period = ATTN_QB + ATTN_WIN
    m = jnp.arange(period)
    tiles = []
    for v in range(ATTN_NVAR):
        rel = v * ATTN_QB - jnp.where(m < ATTN_WIN, m, m - period)
        row = rel_bias.astype(F32)[:, jnp.clip(rel, -MAX_REL, MAX_REL) + MAX_REL]
        flat = jnp.tile(row, (1, ATTN_QB))[:, :ATTN_QB * (period - 1)]
        skew = flat.reshape(n_heads, ATTN_QB, period - 1)[:, :, :ATTN_WIN]
        tiles.append(skew)
    tiles = jnp.stack(tiles)
    i = jnp.arange(ATTN_QB)[None, :, None]
    j = jnp.arange(ATTN_WIN)[None, None, :]
    qpos = (jnp.arange(ATTN_NVAR) * ATTN_QB)[:, None, None] + i
    dchunk = qpos // CHUNK - j // CHUNK
    valid = (dchunk >= 0) & (dchunk <= LEFT_CHUNKS)
    return jnp.where(valid[:, None], tiles, NEG_BIG)


def _attn_kernel(q_ref, k_ref, v_ref, bias_ref, o_ref, *, hb, seq):
    scale = DH_ATTN ** -0.5
    nq = seq // ATTN_QB

    def body(it, carry):
        units = []
        for u in range(ATTN_UNROLL):
            qi = it * ATTN_UNROLL + u
            qs = pl.multiple_of(qi * ATTN_QB, ATTN_QB)
            start = pl.multiple_of(jnp.maximum(qs - LEFT_CHUNKS * CHUNK, 0), ATTN_QB)
            var = jnp.minimum(qi, ATTN_NVAR - 1)
            for h in range(hb):
                units.append((qs, start, var, h, slice(h * DH_ATTN, (h + 1) * DH_ATTN)))
        s_all = [lax.dot_general(q_ref[pl.ds(qs, ATTN_QB), c], k_ref[pl.ds(start, ATTN_WIN), c],
                                 (((1,), (1,)), ((), ())), preferred_element_type=F32)
                 for qs, start, var, h, c in units]
        p_all = []
        for (qs, start, var, h, c), s in zip(units, s_all):
            s = s * scale + bias_ref[var, h]
            m = jnp.max(s, axis=-1, keepdims=True)
            p = jnp.exp(s - m)
            p_all.append((p.astype(BF16), 1.0 / jnp.sum(p, axis=-1, keepdims=True)))
        for (qs, start, var, h, c), (p, inv_l) in zip(units, p_all):
            o = jnp.dot(p, v_ref[pl.ds(start, ATTN_WIN), c], preferred_element_type=F32) * inv_l
            o_ref[pl.ds(qs, ATTN_QB), c] = o.astype(o_ref.dtype)
        return carry

    assert nq % ATTN_UNROLL == 0
    lax.fori_loop(0, nq // ATTN_UNROLL, body, 0)


def attention(qkv, rel_bias, n_heads, *, hb, out_cols, name):
    b, seq, _ = qkv.shape
    assert n_heads % hb == 0 and seq % ATTN_QB == 0 and seq >= ATTN_WIN
    ng = n_heads // hb
    wcols = hb * DH_ATTN
    bias = _attn_bias_tiles(rel_bias)
    kern = functools.partial(_attn_kernel, hb=hb, seq=seq)
    return pl.pallas_call(
        kern,
        grid=(b, ng),
        in_specs=[
            pl.BlockSpec((None, seq, wcols), lambda bi, g: (bi, 0, g)),
            pl.BlockSpec((None, seq, wcols), lambda bi, g: (bi, 0, ng + g)),
            pl.BlockSpec((None, seq, wcols), lambda bi, g: (bi, 0, 2 * ng + g)),
            pl.BlockSpec((ATTN_NVAR, hb, ATTN_QB, ATTN_WIN), lambda bi, g: (0, g, 0, 0)),
        ],
        out_specs=pl.BlockSpec((None, seq, wcols), lambda bi, g: (bi, 0, g)),
        out_shape=jax.ShapeDtypeStruct((b, seq, out_cols), BF16),
        compiler_params=_cparams(2),
        name=name,
    )(qkv, qkv, qkv, bias)


SSM_TILE_GROUPS = LANES // SSM_GROUP
SSM_TILE_STATES = SSM_TILE_GROUPS * SSM_STATE


def _ssm_tables(lam_re, lam_im, log_step, b_re, b_im, c_re, c_im, n_scan_steps):
    g, p = lam_re.shape
    hg = b_re.shape[-1]
    tc = SSM_BLOCK
    tg = SSM_TILE_GROUPS
    nt = g // tg
    lam = lax.complex(jnp.minimum(lam_re.astype(F32), -1e-4), lam_im.astype(F32))
    step = jnp.exp(log_step.astype(F32))[:, None]
    lam_bar = jnp.exp(lam * step)
    b_bar = ((lam_bar - 1.0) / lam)[..., None] * lax.complex(b_re.astype(F32), b_im.astype(F32))
    pw = [jnp.ones_like(lam_bar)]
    for _ in range(tc):
        pw.append(pw[-1] * lam_bar)
    pw = jnp.stack(pw)
    xb = pw[:tc, :, :, None] * b_bar[None]
    xb_re, xb_im = jnp.real(xb), jnp.imag(xb)
    cr, ci = c_re.astype(F32), c_im.astype(F32)
    kmat = (jnp.einsum('ghp,tgpi->tghi', cr, xb_re, precision=HI)
            - jnp.einsum('ghp,tgpi->tghi', ci, xb_im, precision=HI))
    def diag_mask(rows_per_group, cols_per_group, col_groups_stride):
        r = jnp.arange(tg * rows_per_group)[:, None] // rows_per_group
        c = (jnp.arange(col_groups_stride)[None, :] // cols_per_group) % tg
        return r == c
    kc = jnp.transpose(kmat.reshape(tc, nt, tg, hg, hg), (1, 2, 4, 0, 3))
    kc = jnp.tile(kc.reshape(nt, tg * hg, tc, hg), (1, 1, 1, tg)).reshape(nt, tg * hg, tc * tg * hg)
    k_tab = jnp.where(diag_mask(hg, hg, tc * tg * hg), kc, 0.0).astype(BF16)
    k_tab = jnp.pad(k_tab, ((0, 0), (0, 0), (tg * hg, 0)))
    def f_part(xpart):
        xr = jnp.transpose(xpart[::-1].reshape(tc, nt, tg, p, hg), (1, 0, 2, 4, 3))
        xr = jnp.tile(xr.reshape(nt, tc, tg * hg, p), (1, 1, 1, tg))
        return jnp.where(diag_mask(hg, p, tg * p), xr, 0.0).astype(BF16)
    f_tab = jnp.concatenate([f_part(xb_re), f_part(xb_im)], axis=-1)
    gmat = lax.complex(cr, ci)[None] * pw[1:, :, None, :]
    def e_part(gpart):
        gr = jnp.transpose(gpart.reshape(tc, nt, tg, hg, p), (1, 2, 4, 0, 3))
        gr = jnp.tile(gr.reshape(nt, tg * p, tc, hg), (1, 1, 1, tg)).reshape(nt, tg * p, tc * tg * hg)
        return jnp.where(diag_mask(p, hg, tc * tg * hg), gr, 0.0).astype(BF16)
    e_tab = jnp.concatenate([e_part(jnp.real(gmat)), e_part(-jnp.imag(gmat))], axis=1)
    a = pw[tc]
    a1, a2 = [], []
    for _ in range(n_scan_steps):
        ar = jnp.real(a).reshape(nt, tg * p)
        ai = jnp.imag(a).reshape(nt, tg * p)
        a1.append(jnp.concatenate([ar, ar], axis=-1))
        a2.append(jnp.concatenate([-ai, ai], axis=-1))
        a = a * a
    pad = (-n_scan_steps) % 8
    a1 = jnp.pad(jnp.stack(a1, axis=1), ((0, 0), (0, pad), (0, 0)))
    a2 = jnp.pad(jnp.stack(a2, axis=1), ((0, 0), (0, pad), (0, 0)))
    return k_tab, f_tab, e_tab, a1, a2


def _gelu_tanh(x):
    c = math.sqrt(2.0 / math.pi)
    return 0.5 * x * (1.0 + jnp.tanh(c * (x + 0.044715 * (x * x * x))))


def _ssm_kernel(u_ref, k_ref, f_ref, e_ref, a1_ref, a2_ref, d_ref, y_ref, xs_ref, ky_ref, *, nc,
                nsteps):
    tc = SSM_BLOCK
    ns = SSM_TILE_STATES

    def frames(s):
        return u_ref[pl.ds(s, nc, stride=tc), :]

    sloc = None
    for s in range(tc):
        xs = frames(s).astype(BF16)
        xs_ref[s] = xs
        part = jnp.dot(xs, f_ref[s], preferred_element_type=F32)
        sloc = part if sloc is None else sloc + part
    pairs = list(range(0, tc, 2))

    def local_part(t):
        acc = None
        for s in range(t + 2):
            c0 = (t - s + 1) * LANES
            part = jnp.dot(xs_ref[s], k_ref[:, c0:c0 + 2 * LANES], preferred_element_type=F32)
            acc = part if acc is None else acc + part
        ky_ref[:, t * LANES:(t + 2) * LANES] = acc

    x = sloc
    for k in range(max(nsteps, len(pairs))):
        if k < nsteps:
            d = 1 << k
            sh = jnp.concatenate([jnp.zeros((d, 2 * ns), F32), x[:nc - d]], axis=0)
            sw = pltpu.roll(sh, ns, 1)
            x = x + a1_ref[k:k + 1, :] * sh + a2_ref[k:k + 1, :] * sw
        if k < len(pairs):
            local_part(pairs[k])
    sprev = jnp.concatenate([jnp.zeros((1, 2 * ns), F32), x[:nc - 1]], axis=0).astype(BF16)
    dskip = d_ref[...]
    for t in pairs:
        acc = ky_ref[:, t * LANES:(t + 2) * LANES] + jnp.dot(
            sprev, e_ref[:, t * LANES:(t + 2) * LANES], preferred_element_type=F32)
        for j in range(2):
            y = acc[:, j * LANES:(j + 1) * LANES] + dskip * frames(t + j)
            y_ref[pl.ds(t + j, nc, stride=tc), :] = _gelu_tanh(y)


def ssm_gelu(u, lam_re, lam_im, log_step, b_re, b_im, c_re, c_im, d_skip, *, name):
    b, seq, ds = u.shape
    g = lam_re.shape[0]
    tc = SSM_BLOCK
    nc = seq // tc
    assert seq % tc == 0 and ds == g * SSM_GROUP and ds % LANES == 0 and nc & (nc - 1) == 0
    assert lam_re.shape[1] == SSM_STATE
    nt = ds // LANES
    nsteps = nc.bit_length() - 1
    k_tab, f_tab, e_tab, a1, a2 = _ssm_tables(lam_re, lam_im, log_step, b_re, b_im, c_re, c_im, nsteps)
    ns2 = 2 * SSM_TILE_STATES
    sp = a1.shape[1]
    kern = functools.partial(_ssm_kernel, nc=nc, nsteps=nsteps)
    return pl.pallas_call(
        kern,
        grid=(nt, b),
        in_specs=[
            pl.BlockSpec((None, seq, LANES), lambda l, bi: (bi, 0, l)),
            pl.BlockSpec((None, LANES, (tc + 1) * LANES), lambda l, bi: (l, 0, 0)),
            pl.BlockSpec((None, tc, LANES, ns2), lambda l, bi: (l, 0, 0, 0)),
            pl.BlockSpec((None, ns2, tc * LANES), lambda l, bi: (l, 0, 0)),
            pl.BlockSpec((None, sp, ns2), lambda l, bi: (l, 0, 0)),
            pl.BlockSpec((None, sp, ns2), lambda l, bi: (l, 0, 0)),
            pl.BlockSpec((1, LANES), lambda l, bi: (0, l)),
        ],
        out_specs=pl.BlockSpec((None, seq, LANES), lambda l, bi: (bi, 0, l)),
        out_shape=jax.ShapeDtypeStruct((b, seq, ds), F32),
        scratch_shapes=[pltpu.VMEM((tc, nc, LANES), BF16), pltpu.VMEM((nc, tc * LANES), F32)],
        compiler_params=_cparams(2),
        name=name,
    )(u, k_tab, f_tab, e_tab, a1, a2, d_skip.reshape(1, ds).astype(F32))


def _glu_kernel(y_ref, w_ref, b_ref, dst_ref, o_ref):
    del dst_ref
    yb = y_ref[...]
    z = jnp.dot(yb.astype(BF16), w_ref[...], preferred_element_type=F32) + b_ref[...]
    o_ref[...] = (yb * _sigmoid(z)).astype(o_ref.dtype)


def glu(y, w_glu, b_glu, dst, *, tm, name):
    m, d = y.shape
    assert m % tm == 0 and dst.shape[0] == m and dst.shape[1] % d == 0 and dst.dtype == BF16
    col_blk = dst.shape[1] // d - 1
    return pl.pallas_call(
        _glu_kernel,
        grid=(m // tm,),
        in_specs=[
            pl.BlockSpec((tm, d), lambda i: (i, 0)),
            pl.BlockSpec((d, d), lambda i: (0, 0)),
            pl.BlockSpec((1, d), lambda i: (0, 0)),
            pl.BlockSpec(memory_space=pl.ANY),
        ],
        out_specs=pl.BlockSpec((tm, d), lambda i: (i, col_blk)),
        out_shape=jax.ShapeDtypeStruct(dst.shape, BF16),
        input_output_aliases={3: 0},
        compiler_params=_cparams(1),
        name=name,
    )(y, w_glu, b_glu.reshape(1, d).astype(F32), dst)


def _softplus(x):
    return jnp.maximum(x, 0.0) + jnp.log(1.0 + jnp.exp(-jnp.abs(x)))


def _cumsum_rows(x, n):
    rows = lax.broadcasted_iota(jnp.int32, x.shape, 0)
    d = 1
    while d < n:
        x = x + jnp.where(rows >= d, pltpu.roll(x, d, 0), 0.0)
        d *= 2
    return x


def _cumsum_lane_segments(x, seg):
    pos = lax.broadcasted_iota(jnp.int32, x.shape, 1) & (seg - 1)
    d = 1
    while d < seg:
        x = x + jnp.where(pos >= d, pltpu.roll(x, d, 1), 0.0)
        d *= 2
    return x


DELTA_PACK = 4
DELTA_ROWS = DELTA_PACK * CHUNK
DELTA_LEVELS = CHUNK.bit_length() - 1


def _delta_level_map():
    import numpy as np
    i = np.arange(DELTA_ROWS)[:, None]
    j = np.arange(DELTA_ROWS)[None, :]
    x = i ^ j
    lev = np.zeros_like(x)
    for a in range(1, 8):
        lev += (x >> a) != 0
    out = np.where((i > j) & ((i // CHUNK) == (j // CHUNK)), lev, 99)
    out = np.where(i == j, -1, out)
    return out.astype(np.int32)


def _delta_kernel(q_ref, k_ref, v_ref, gate_ref, bgc_ref, gr_ref, cwq_ref, cwk_ref, cwv_ref,
                  pc_ref, pr_ref, nw_ref, lev_ref, o_ref, s_ref, halo_ref, *, hb, tb):
    c = CHUNK
    dk = DK_DELTA
    pk = DELTA_PACK
    rows = DELTA_ROWS
    sub = hb // pk
    sw = pk * dk
    scale = dk ** -0.5
    n_chunks = tb // c
    ti = pl.program_id(2)

    @pl.when(ti == 0)
    def _():
        s_ref[...] = jnp.zeros_like(s_ref)
        halo_ref[...] = jnp.zeros_like(halo_ref)

    alog_c = pc_ref[0:1, :]
    dtb_c = pc_ref[1:2, :]
    alog_r = pr_ref[0:sub, :]
    dtb_r = pr_ref[sub:2 * sub, :]
    nw = nw_ref[...]

    def conv_silu(tail, cur, cw_ref, lanes):
        win = jnp.concatenate([tail, cur], axis=0)
        acc = (cw_ref[3:4, lanes] * win[8:8 + c] + cw_ref[2:3, lanes] * win[7:7 + c]
               + cw_ref[1:2, lanes] * win[6:6 + c] + cw_ref[0:1, lanes] * win[5:5 + c])
        return acc * _sigmoid(acc)

    def stack_heads(x):
        return jnp.concatenate([x[:, j * dk:(j + 1) * dk] for j in range(pk)], axis=0)

    def stack_cols(x, s, off):
        return jnp.concatenate(
            [x[:, off + pk * s + j:off + pk * s + j + 1] for j in range(pk)], axis=0)

    def chunk_stages(ch, get_tails):
        r0 = pl.multiple_of(ch * c, c)
        bg = bgc_ref[pl.ds(r0, c), :]
        beta_all = _sigmoid(bg)
        g_all = -jnp.exp(alog_c) * _softplus(bg + dtb_c)
        gc_all = _cumsum_rows(g_all, c)
        gl_all = jnp.broadcast_to(gc_all[c - 1:c, :], gc_all.shape)
        g_row = -jnp.exp(alog_r) * _softplus(gr_ref[ch] + dtb_r)
        gcr_all = _cumsum_lane_segments(g_row, c)
        lev = lev_ref[...]
        ri = lax.broadcasted_iota(jnp.int32, (rows, pk * dk), 0)
        cj = lax.broadcasted_iota(jnp.int32, (rows, pk * dk), 1)
        head_block = (ri // c) == (cj // dk)
        subs = [dict() for _ in range(sub)]

        def prep(s, st):
            tq, tk, tv = get_tails()
            lanes = slice(s * sw, (s + 1) * sw)
            curq = q_ref[pl.ds(r0, c), lanes].astype(F32)
            curk = k_ref[pl.ds(r0, c), lanes].astype(F32)
            curv = v_ref[pl.ds(r0, c), lanes].astype(F32)
            st["tails"] = (curq[c - 8:], curk[c - 8:], curv[c - 8:])
            q = stack_heads(conv_silu(tq[:, lanes], curq, cwq_ref, lanes))
            k = stack_heads(conv_silu(tk[:, lanes], curk, cwk_ref, lanes))
            v = stack_heads(conv_silu(tv[:, lanes], curv, cwv_ref, lanes))
            q = q * lax.rsqrt(jnp.sum(q * q, axis=-1, keepdims=True) + NORM_EPS)
            k = k * lax.rsqrt(jnp.sum(k * k, axis=-1, keepdims=True) + NORM_EPS)
            qs = q * scale
            beta = stack_cols(beta_all, s, 0)
            gc_col = stack_cols(gc_all, s, hb)
            gl_col = stack_cols(gl_all, s, hb)
            gc_row = gcr_all[s:s + 1, :]
            egc = jnp.exp(gc_col)
            kb = k * beta
            lhs = jnp.concatenate([kb, qs], axis=0).astype(BF16)
            kq = lax.dot_general(lhs, k.astype(BF16), (((1,), (1,)), ((), ())),
                                 preferred_element_type=F32)
            dec = jnp.exp(gc_col - gc_row)
            nm = kq[:rows] * dec
            st.update(
                nm=nm, qg=qs * egc,
                attn=jnp.where(lev < DELTA_LEVELS, kq[rows:] * dec, 0.0),
                x=jnp.concatenate([v * beta, kb * egc], axis=1),
                kd=(k * jnp.exp(gl_col - gc_col)).astype(BF16),
                t=jnp.where(lev == 0, -nm, jnp.where(lev == -1, 1.0, 0.0)))

        def level(a):
            def run(s, st):
                cm = jnp.where(lev == a, st["nm"], 0.0).astype(BF16)
                tb16 = st["t"].astype(BF16)
                pm = jnp.dot(tb16, cm, preferred_element_type=F32)
                st["t"] = st["t"] - jnp.dot(pm.astype(BF16), tb16, preferred_element_type=F32)
            return run

        def solve(s, st):
            t_off = jnp.where(lev == -1, 0.0, st["t"]).astype(BF16)
            x = st["x"]
            st["x"] = x + jnp.dot(t_off, x.astype(BF16), preferred_element_type=F32)

        def read_state(s, st):
            w_c = st["x"][:, dk:]
            qg = st["qg"]
            ws_parts, qs_parts = [], []
            for j in range(pk):
                h = pk * s + j
                rs = slice(j * c, (j + 1) * c)
                lhs2 = jnp.concatenate([w_c[rs], qg[rs]], axis=0).astype(BF16)
                ws = jnp.dot(lhs2, s_ref[h].astype(BF16), preferred_element_type=F32)
                ws_parts.append(ws[:c])
                qs_parts.append(ws[c:])
            st["v_new"] = st["x"][:, :dk] - jnp.concatenate(ws_parts, axis=0)
            st["qs"] = jnp.concatenate(qs_parts, axis=0)

        def outputs(s, st):
            st["out"] = st["qs"] + jnp.dot(st["attn"].astype(BF16), st["v_new"].astype(BF16),
                                           preferred_element_type=F32)
            vbd = jnp.where(head_block, jnp.concatenate([st["v_new"]] * pk, axis=1), 0.0)
            st["upd"] = lax.dot_general(st["kd"], vbd.astype(BF16), (((0,), (0,)), ((), ())),
                                        preferred_element_type=F32)

        def finish(s, st):
            out = st["out"]
            o = out * lax.rsqrt(jnp.mean(out * out, axis=-1, keepdims=True) + NORM_EPS) * nw
            gt = stack_heads(gate_ref[pl.ds(r0, c), s * sw:(s + 1) * sw].astype(F32))
            res = (o * (gt * _sigmoid(gt))).astype(o_ref.dtype)
            for j in range(pk):
                h = pk * s + j
                e_last = jnp.exp(gc_all[c - 1:c, hb + h:hb + h + 1])
                s_ref[h] = s_ref[h] * e_last + st["upd"][:, j * dk:(j + 1) * dk]
                o_ref[pl.ds(r0, c), h * dk:(h + 1) * dk] = res[j * c:(j + 1) * c]

        stages = [prep] + [level(a) for a in range(1, DELTA_LEVELS)] + [solve, read_state, outputs,
                                                                        finish]
        bound = [functools.partial(lambda f, s: f(s, subs[s]), f) for f in stages]
        tails_out = lambda: tuple(
            jnp.concatenate([st["tails"][i] for st in subs], axis=1) for i in range(3))
        return bound, tails_out

    def pair_body(it, tails):
        st_a, tails_a = chunk_stages(2 * it, lambda: tails)
        st_b, tails_b = chunk_stages(2 * it + 1, tails_a)
        n_lev = DELTA_LEVELS - 1
        prep_a, lev_a, rest_a = st_a[0], st_a[1:1 + n_lev], st_a[1 + n_lev:]
        prep_b, lev_b, rest_b = st_b[0], st_b[1:1 + n_lev], st_b[1 + n_lev:]
        for s in range(sub):
            prep_a(s)
        for a in range(n_lev):
            for s in range(sub):
                lev_a[a](s)
            for s in range(a * sub // n_lev, (a + 1) * sub // n_lev):
                prep_b(s)
        for s in range(sub):
            rest_a[0](s)
        for k, f_a in enumerate(rest_a[1:]):
            for s in range(sub):
                f_a(s)
                lev_b[k](s)
        for f_b in lev_b[len(rest_a) - 1:] + rest_b:
            for s in range(sub):
                f_b(s)
        return tails_b()

    assert n_chunks % 2 == 0
    tails0 = (halo_ref[0], halo_ref[1], halo_ref[2])
    tq, tk, tv = lax.fori_loop(0, n_chunks // 2, pair_body, tails0)
    halo_ref[0] = tq
    halo_ref[1] = tk
    halo_ref[2] = tv


def gated_deltanet(proj, braw, conv_w, a_log, dt_bias, norm_w, n_heads, *, hb, tb, name):
    b, seq, _ = proj.shape
    dk = DK_DELTA
    c = CHUNK
    pk = DELTA_PACK
    assert n_heads % hb == 0 and hb % pk == 0 and seq % tb == 0 and tb % c == 0
    ng = n_heads // hb
    sub = hb // pk
    w = hb * dk
    nct = seq // c
    br = braw.reshape(b, seq, 2, ng, hb)
    bgc = jnp.transpose(br, (0, 3, 1, 2, 4)).reshape(b, ng, seq, 2 * hb)
    araw = braw[..., n_heads:].reshape(b, nct, c, ng, sub, pk)
    gr = jnp.transpose(araw, (0, 3, 1, 4, 5, 2)).reshape(b, ng, nct, sub, pk * c)
    al = a_log.astype(F32).reshape(ng, hb)
    dt = dt_bias.astype(F32).reshape(ng, hb)
    zeros = jnp.zeros((ng, hb), F32)
    pc = jnp.stack([jnp.concatenate([zeros, al], axis=1),
                    jnp.concatenate([zeros, dt], axis=1)], axis=1)
    rep = lambda p: jnp.repeat(p.reshape(ng, sub, pk), c, axis=-1)
    pr = jnp.concatenate([rep(al), rep(dt)], axis=1)
    cw = conv_w.astype(F32)
    lev = jnp.asarray(_delta_level_map())
    rows = pk * c
    kern = functools.partial(_delta_kernel, hb=hb, tb=tb)
    return pl.pallas_call(
        kern,
        grid=(b, ng, seq // tb),
        in_specs=[
            pl.BlockSpec((None, tb, w), lambda bi, g, t: (bi, t, g)),
            pl.BlockSpec((None, tb, w), lambda bi, g, t: (bi, t, ng + g)),
            pl.BlockSpec((None, tb, w), lambda bi, g, t: (bi, t, 2 * ng + g)),
            pl.BlockSpec((None, tb, w), lambda bi, g, t: (bi, t, 3 * ng + g)),
            pl.BlockSpec((None, None, tb, 2 * hb), lambda bi, g, t: (bi, g, t, 0)),
            pl.BlockSpec((None, None, tb // c, sub, rows), lambda bi, g, t: (bi, g, t, 0, 0)),
            pl.BlockSpec((CONV_K, w), lambda bi, g, t: (0, g)),
            pl.BlockSpec((CONV_K, w), lambda bi, g, t: (0, ng + g)),
            pl.BlockSpec((CONV_K, w), lambda bi, g, t: (0, 2 * ng + g)),
            pl.BlockSpec((None, 2, 2 * hb), lambda bi, g, t: (g, 0, 0)),
            pl.BlockSpec((None, 2 * sub, rows), lambda bi, g, t: (g, 0, 0)),
            pl.BlockSpec((1, dk), lambda bi, g, t: (0, 0)),
            pl.BlockSpec((rows, rows), lambda bi, g, t: (0, 0)),
        ],
        out_specs=pl.BlockSpec((None, tb, w), lambda bi, g, t: (bi, t, g)),
        out_shape=jax.ShapeDtypeStruct((b, seq, n_heads * dk), BF16),
        scratch_shapes=[pltpu.VMEM((hb, dk, dk), F32), pltpu.VMEM((3, 8, w), F32)],
        compiler_params=_cparams(3),
        name=name,
    )(proj, proj, proj, proj, bgc, gr, cw, cw, cw, pc, pr, norm_w.astype(F32).reshape(1, dk), lev)


FFN_UP_TN = 256


def _ffn(x32, x16, w_gate, w_up, w_down, gain, bias, layer, last=False):
    dff = w_gate.shape[2]
    dff_pad = -(-dff // FFN_UP_TN) * FFN_UP_TN
    h = swiglu_up(x16, w_gate, w_up, layer, n_out=dff_pad, tm=2048, tn=FFN_UP_TN, name=f"ffn_up_{layer}")
    wd = jnp.pad(w_down[layer].astype(BF16), ((0, dff_pad - dff), (0, 0)))[None]
    y = matmul(h, wd, 0, F32, col0=0, n=wd.shape[2], tm=512, tn=512, resid=x32, alpha=ALPHA,
               x_buffers=2, name=f"ffn_down_{layer}")
    return layer_norm(y, gain, bias, tm=256, with_bf16=not last, name=f"ffn_ln_{layer}")


def kernel(x, ab_w_in, ab_rel_bias, ab_lam_re, ab_lam_im, ab_log_step, ab_b_re, ab_b_im, ab_c_re, ab_c_im, ab_d_skip, ab_w_glu, ab_b_glu, ab_w_out, c_w_in, c_conv, c_a_log, c_dt_bias, c_norm_w, c_w_out, ffn_w_gate, ffn_w_up, ffn_w_down, ln_gain, ln_bias):
    b, seq, d = x.shape
    t = b * seq
    x32 = x.reshape(t, d).astype(F32)
    x16 = x32.astype(BF16)

    n_heads_a = ab_rel_bias.shape[1]
    d_attn = n_heads_a * DH_ATTN
    d_ssm = ab_w_in.shape[2] - 3 * d_attn
    qkv = matmul(x16, ab_w_in, 0, BF16, col0=0, n=3 * d_attn, tm=2048, tn=512, name="ab_in_qkv")
    u = matmul(x16, ab_w_in, 0, F32, col0=3 * d_attn, n=d_ssm, tm=2048, tn=512, name="ab_in_u")
    y_ab = attention(qkv.reshape(b, seq, 3 * d_attn), ab_rel_bias[0], n_heads_a, hb=2,
                     out_cols=d_attn + d_ssm, name="ab_attn")
    y_s = ssm_gelu(u.reshape(b, seq, d_ssm), ab_lam_re[0], ab_lam_im[0], ab_log_step[0], ab_b_re[0],
                   ab_b_im[0], ab_c_re[0], ab_c_im[0], ab_d_skip[0].reshape(-1), name="ab_ssm")
    y_ab = glu(y_s.reshape(t, d_ssm), ab_w_glu[0].astype(BF16), ab_b_glu[0],
               y_ab.reshape(t, d_attn + d_ssm), tm=512, name="ab_glu")
    y = matmul(y_ab, ab_w_out, 0, F32, col0=0, n=d, tm=1024, tn=512, resid=x32, alpha=ALPHA,
               x_buffers=2, name="ab_out")
    x32, x16 = layer_norm(y, ln_gain[0, 0], ln_bias[0, 0], tm=256, name="ab_ln")
    x32, x16 = _ffn(x32, x16, ffn_w_gate, ffn_w_up, ffn_w_down, ln_gain[0, 1], ln_bias[0, 1], 0)

    n_heads_c = c_a_log.shape[1]
    d_delta = n_heads_c * DK_DELTA
    cw_t = jnp.swapaxes(c_w_in, 1, 2)
    proj = matmul(x16, cw_t, 0, BF16, col0=0, n=4 * d_delta, tm=2048, tn=512, transposed=True,
                  name="c_in")
    braw = matmul(x16, cw_t[:, 4 * d_delta:], 0, F32, col0=0, n=2 * n_heads_c, tm=2048,
                  tn=2 * n_heads_c, transposed=True, name="c_in_small")
    o = gated_deltanet(proj.reshape(b, seq, 4 * d_delta), braw.reshape(b, seq, 2 * n_heads_c),
                       c_conv[0], c_a_log[0], c_dt_bias[0], c_norm_w[0], n_heads_c,
                       hb=16, tb=512, name="c_delta")
    y = matmul(o.reshape(t, d_delta), c_w_out, 0, F32, col0=0, n=d, tm=1024, tn=512, resid=x32,
               alpha=ALPHA, x_buffers=2, name="c_out")
    x32, x16 = layer_norm(y, ln_gain[1, 0], ln_bias[1, 0], tm=256, name="c_ln")
    (x32,) = _ffn(x32, x16, ffn_w_gate, ffn_w_up, ffn_w_down, ln_gain[1, 1], ln_bias[1, 1], 1,
                  last=True)
    return x32.reshape(b, seq, d).astype(x.dtype)
```
